```python
import jax, jax.numpy as jnp
from jax import lax
import numpy as np


D_MODEL = 1024
BATCH = 16
SEQ = 2048
DEPTH = 2

HEAD_DIM = 64
EPS = 1e-6
GLA_HEADS = 4
GLA_DV = D_MODEL // (2 * GLA_HEADS)
GLA_DK = GLA_DV // 2
GLA_RANK = 16
GLA_TAU = 16.0
GLA_CHUNK = 64
DSW_HEADS = D_MODEL // (2 * HEAD_DIM)
DSW_PATTERNS = ((128, 1), (512, 4), (2048, 16))
DSW_BLOCK = 128
CONV_CH = D_MODEL // 2
CONV_WIDTH = 31
SB_HEADS = D_MODEL // (2 * HEAD_DIM)
SB_BLOCK = 128
D_FF = ((8 * D_MODEL // 3 + 127) // 128) * 128
FFN_CONV = 3
ROPE_THETA = 500000.0
ROPE_DIMS = HEAD_DIM // 4
IN0_WIDTH = 2 * GLA_HEADS * GLA_DK + 2 * GLA_HEADS * GLA_DV + GLA_RANK + 3 * DSW_HEADS * HEAD_DIM
IN1_WIDTH = 2 * CONV_CH + 3 * SB_HEADS * HEAD_DIM
OUT0_WIDTH = GLA_HEADS * GLA_DV + DSW_HEADS * HEAD_DIM
OUT1_WIDTH = CONV_CH + SB_HEADS * HEAD_DIM

kernel_name = 'hybrid_gla_dilated_conformer_stickbreak'


def _offsets(*sizes):
    out, acc = [], 0
    for s in sizes[:-1]:
        acc += s
        out.append(acc)
    return out


def _rmsnorm(x, g):
    xf = x.astype(jnp.float32)
    y = xf * lax.rsqrt(jnp.mean(xf * xf, axis=-1, keepdims=True) + EPS)
    return (y * g).astype(x.dtype)


def _layernorm(x, g, b):
    xf = x.astype(jnp.float32)
    mu = jnp.mean(xf, axis=-1, keepdims=True)
    var = jnp.mean(jnp.square(xf - mu), axis=-1, keepdims=True)
    return ((xf - mu) * lax.rsqrt(var + EPS) * g + b).astype(x.dtype)


def _heads(t, n):
    B, S, W = t.shape
    return t.reshape(B, S, n, W // n).transpose(0, 2, 1, 3)


def _merge_heads(t):
    B, n, S, d = t.shape
    return t.transpose(0, 2, 1, 3).reshape(B, S, n * d)


def _causal_depthwise_conv(x, w):
    K, C = w.shape
    return lax.conv_general_dilated(
        x, w[:, None, :].astype(x.dtype), window_strides=(1,), padding=[(K - 1, 0)],
        dimension_numbers=('NWC', 'WIO', 'NWC'), feature_group_count=C)


def _rope_partial(x):
    S = x.shape[-2]
    half = ROPE_DIMS // 2
    inv = ROPE_THETA ** (-jnp.arange(half, dtype=jnp.float32) / half)
    ang = jnp.arange(S, dtype=jnp.float32)[:, None] * inv[None, :]
    cos, sin = jnp.cos(ang).astype(x.dtype), jnp.sin(ang).astype(x.dtype)
    x1, x2, xp = x[..., :half], x[..., half:ROPE_DIMS], x[..., ROPE_DIMS:]
    return jnp.concatenate([x1 * cos - x2 * sin, x2 * cos + x1 * sin, xp], axis=-1)


def _gla(q, k, v, log_a):
    B, H, S, DK = q.shape
    DV = v.shape[-1]
    C = GLA_CHUNK
    N = S // C
    f32 = jnp.float32
    rs = lambda t: t.astype(f32).reshape(B, H, N, C, t.shape[-1])
    qc, kc, vc, ac = rs(q * DK ** -0.5), rs(k), rs(v), rs(log_a)
    bcum = jnp.cumsum(ac, axis=3)
    btot = bcum[..., -1, :]
    q_dec = qc * jnp.exp(bcum)
    k_inv = kc * jnp.exp(-bcum)
    k_tail = kc * jnp.exp(btot[..., None, :] - bcum)
    causal = jnp.tril(jnp.ones((C, C), dtype=bool))
    scores = jnp.where(causal, jnp.einsum('bhncd,bhnsd->bhncs', q_dec, k_inv), 0.0)
    o_intra = jnp.einsum('bhncs,bhnse->bhnce', scores, vc)
    kv = jnp.einsum('bhnsd,bhnse->bhnde', k_tail, vc)

    def step(state, inp):
        dec, kv_n = inp
        return state * dec[..., None] + kv_n, state

    init = jnp.zeros((B, H, DK, DV), f32)
    _, states = lax.scan(step, init, (jnp.moveaxis(jnp.exp(btot), 2, 0), jnp.moveaxis(kv, 2, 0)))
    states = jnp.moveaxis(states, 0, 2)
    o_inter = jnp.einsum('bhncd,bhnde->bhnce', q_dec, states)
    return (o_intra + o_inter).reshape(B, H, S, DV)


def _dilated_branch(q, k, v, window, dilation):
    B, H, S, D = q.shape
    L = S // dilation
    W = window // dilation
    C = DSW_BLOCK
    nb = -(-L // C)
    pad = nb * C - L

    def to_blocks(t):
        t = t.reshape(B, H, L, dilation, D).swapaxes(2, 3)
        t = jnp.pad(t, ((0, 0), (0, 0), (0, 0), (0, pad), (0, 0)))
        return t.reshape(B, H, dilation, nb, C, D)

    def with_prev(t):
        prev = jnp.pad(t[:, :, :, :-1], ((0, 0), (0, 0), (0, 0), (1, 0), (0, 0), (0, 0)))
        return jnp.concatenate([prev, t], axis=4)

    qb = to_blocks(q)
    kc, vc = with_prev(to_blocks(k)), with_prev(to_blocks(v))
    s = jnp.einsum('bhrnqd,bhrnkd->bhrnqk', qb, kc).astype(jnp.float32) * (D ** -0.5)
    qi = jnp.arange(C)[:, None]
    ki = jnp.arange(2 * C)[None, :]
    dist = qi + C - ki
    blk = jnp.arange(nb)[:, None, None]
    valid = (dist >= 0) & (dist <= W) & (blk * C + ki - C >= 0)
    s = jnp.where(valid, s, -jnp.inf)
    m = jnp.max(s, axis=-1, keepdims=True)
    p = jnp.exp(s - m)
    den = jnp.sum(p, axis=-1, keepdims=True)
    num = jnp.einsum('bhrnqk,bhrnkd->bhrnqd', p, vc.astype(jnp.float32))

    def from_blocks(t):
        e = t.shape[-1]
        t = t.reshape(B, H, dilation, nb * C, e)[:, :, :, :L]
        return t.swapaxes(2, 3).reshape(B, H, S, e)

    return from_blocks(num), from_blocks(m), from_blocks(den)


def _dilated_window_attention(q, k, v):
    branches = [_dilated_branch(q, k, v, w, r) for (w, r) in DSW_PATTERNS]
    nums = jnp.stack([b[0] for b in branches])
    ms = jnp.stack([b[1] for b in branches])
    dens = jnp.stack([b[2] for b in branches])
    wts = jnp.exp(ms - jnp.max(ms, axis=0, keepdims=True))
    return jnp.sum(nums * wts, axis=0) / jnp.sum(dens * wts, axis=0)


def _stick_breaking_attention(q, k, v):
    B, H, S, D = q.shape
    nb = S // SB_BLOCK
    qb = q.reshape(B, H, nb, SB_BLOCK, D).transpose(2, 0, 1, 3, 4)
    pos_k = jnp.arange(S)
    vf = v.astype(jnp.float32)

    def block(args):
        n, qn = args
        z = jnp.einsum('bhqd,bhkd->bhqk', qn, k).astype(jnp.float32) * (D ** -0.5)
        pos_q = n * SB_BLOCK + jnp.arange(SB_BLOCK)
        valid = pos_k[None, :] < pos_q[:, None]
        log_beta = jax.nn.log_sigmoid(z)
        log_1m = jnp.where(valid, log_beta - z, 0.0)
        after = lax.cumsum(log_1m, axis=3, reverse=True) - log_1m
        a = jnp.where(valid, jnp.exp(log_beta + after), 0.0)
        return jnp.einsum('bhqk,bhkd->bhqd', a, vf)

    out = lax.map(block, (jnp.arange(nb), qb))
    return out.transpose(1, 2, 0, 3, 4).reshape(B, H, S, D)


def _mixer_ab(h, w_in, gla_wa2, gla_ba, gla_norm, w_out):
    aw_k, aw_v, bw = GLA_HEADS * GLA_DK, GLA_HEADS * GLA_DV, DSW_HEADS * HEAD_DIM
    aq, ak, av, ag, ar, bq, bk, bv = jnp.split(
        h @ w_in, _offsets(aw_k, aw_k, aw_v, aw_v, GLA_RANK, bw, bw, bw), axis=-1)
    log_a = jax.nn.log_sigmoid((ar @ gla_wa2 + gla_ba).astype(jnp.float32)) / GLA_TAU
    oa = _gla(_heads(aq, GLA_HEADS), _heads(ak, GLA_HEADS), _heads(av, GLA_HEADS), _heads(log_a, GLA_HEADS))
    oa = _merge_heads(_rmsnorm(oa, gla_norm)).astype(h.dtype) * jax.nn.silu(ag)
    ob = _dilated_window_attention(_rope_partial(_heads(bq, DSW_HEADS)),
                                   _rope_partial(_heads(bk, DSW_HEADS)),
                                   _heads(bv, DSW_HEADS))
    ob = _merge_heads(ob).astype(h.dtype)
    return jnp.concatenate([oa, ob], axis=-1) @ w_out


def _mixer_cd(h, w_in, conv_w, conv_b, ln_g, ln_b, w_out):
    sw = SB_HEADS * HEAD_DIM
    ca, cb, dq, dk, dv = jnp.split(h @ w_in, _offsets(CONV_CH, CONV_CH, sw, sw, sw), axis=-1)
    c = ca * jax.nn.sigmoid(cb)
    c = _causal_depthwise_conv(c, conv_w) + conv_b
    c = jax.nn.silu(_layernorm(c, ln_g, ln_b))
    od = _stick_breaking_attention(_heads(dq, SB_HEADS), _heads(dk, SB_HEADS), _heads(dv, SB_HEADS))
    od = _merge_heads(od).astype(h.dtype)
    return jnp.concatenate([c, od], axis=-1) @ w_out


def _conv_ffn(h, w_up, w_conv, w_down):
    u = _causal_depthwise_conv(h @ w_up, w_conv)
    g, val = jnp.split(u, 2, axis=-1)
    return (jax.nn.silu(g) * val) @ w_down


def setup_inputs(seed: int = 0) -> dict:
    key = jax.random.key(seed)
    ks = jax.random.split(key, 32)
    f32 = jnp.float32

    def nrm(k, shape, scale):
        return jax.random.normal(k, shape, f32) * scale

    def gain(k, n):
        return 1.0 + 0.01 * jax.random.normal(k, (n,), f32)

    out_scale = (2 * DEPTH) ** -0.5
    return {
        'x': nrm(ks[0], (BATCH, SEQ, D_MODEL), 1.0),
        'norm_mix0': gain(ks[1], D_MODEL),
        'w_in0': nrm(ks[2], (D_MODEL, IN0_WIDTH), D_MODEL ** -0.5),
        'gla_wa2': nrm(ks[3], (GLA_RANK, GLA_HEADS * GLA_DK), GLA_RANK ** -0.5),
        'gla_ba': nrm(ks[4], (GLA_HEADS * GLA_DK,), 0.01),
        'gla_norm': gain(ks[5], GLA_DV),
        'w_out0': nrm(ks[6], (OUT0_WIDTH, D_MODEL), OUT0_WIDTH ** -0.5 * out_scale),
        'norm_ffn0': gain(ks[7], D_MODEL),
        'ffn_up0': nrm(ks[8], (D_MODEL, 2 * D_FF), D_MODEL ** -0.5),
        'ffn_conv0': nrm(ks[9], (FFN_CONV, 2 * D_FF), FFN_CONV ** -0.5),
        'ffn_down0': nrm(ks[10], (D_FF, D_MODEL), D_FF ** -0.5 * out_scale),
        'norm_mix1': gain(ks[11], D_MODEL),
        'w_in1': nrm(ks[12], (D_MODEL, IN1_WIDTH), D_MODEL ** -0.5),
        'conv_w1': nrm(ks[13], (CONV_WIDTH, CONV_CH), CONV_WIDTH ** -0.5),
        'conv_b1': nrm(ks[14], (CONV_CH,), 0.01),
        'conv_ln_g1': gain(ks[15], CONV_CH),
        'conv_ln_b1': nrm(ks[16], (CONV_CH,), 0.01),
        'w_out1': nrm(ks[17], (OUT1_WIDTH, D_MODEL), OUT1_WIDTH ** -0.5 * out_scale),
        'norm_ffn1': gain(ks[18], D_MODEL),
        'ffn_up1': nrm(ks[19], (D_MODEL, 2 * D_FF), D_MODEL ** -0.5),
        'ffn_conv1': nrm(ks[20], (FFN_CONV, 2 * D_FF), FFN_CONV ** -0.5),
        'ffn_down1': nrm(ks[21], (D_FF, D_MODEL), D_FF ** -0.5 * out_scale),
        'final_norm': gain(ks[22], D_MODEL),
    }


def reference(x, norm_mix0, w_in0, gla_wa2, gla_ba, gla_norm, w_out0, norm_ffn0, ffn_up0, ffn_conv0, ffn_down0,
              norm_mix1, w_in1, conv_w1, conv_b1, conv_ln_g1, conv_ln_b1, w_out1, norm_ffn1, ffn_up1, ffn_conv1,
              ffn_down1, final_norm):
    layers = (
        (norm_mix0, (w_in0, gla_wa2, gla_ba, gla_norm, w_out0), (norm_ffn0, ffn_up0, ffn_conv0, ffn_down0)),
        (norm_mix1, (w_in1, conv_w1, conv_b1, conv_ln_g1, conv_ln_b1, w_out1), (norm_ffn1, ffn_up1, ffn_conv1, ffn_down1)),
    )
    h = x
    for i in range(DEPTH):
        g_mix, mix_p, (g_ffn, up, cw, down) = layers[i]
        mixer = _mixer_ab if i % 2 == 0 else _mixer_cd
        h = h + mixer(_rmsnorm(h, g_mix), *mix_p)
        h = h + _conv_ffn(_rmsnorm(h, g_ffn), up, cw, down)
    return _rmsnorm(h, final_norm)
```

```python
import functools

import jax
import jax.numpy as jnp
from jax import lax
from jax.experimental import pallas as pl
from jax.experimental.pallas import tpu as pltpu

F32 = jnp.float32
BF16 = jnp.bfloat16

D_MODEL = 1024
HEAD_DIM = 64
EPS = 1e-6
GLA_HEADS = 4
GLA_DV = 128
GLA_DK = 64
GLA_RANK = 16
GLA_TAU = 16.0
GLA_CHUNK = 64
DSW_HEADS = 8
DSW_PATTERNS = ((128, 1), (512, 4), (2048, 16))
DSW_BLOCK = 128
CONV_CH = 512
CONV_WIDTH = 31
SB_HEADS = 8
D_FF = 2816
FFN_CONV = 3
ROPE_THETA = 500000.0
ROPE_DIMS = 16

LANES = 128
SUBLANES = 8
VMEM_LIMIT = 56 * 1024 * 1024
MASK_NEG = -1e30

PROJ_TM = 512
FFN_TM = 512
FFN_TF = 256
CONV_TS = 256
CONV_HALO = 32
CONV_RB = 32
SB_TQ = 256
SB_TK = 256


def _dot(a, b):
    return jnp.dot(a, b, preferred_element_type=F32)


def _dot_nt(a, b):
    return lax.dot_general(a, b, (((1,), (1,)), ((), ())), preferred_element_type=F32)


def _dot_tn(a, b):
    return lax.dot_general(a, b, (((0,), (0,)), ((), ())), preferred_element_type=F32)


def _rmsnorm(x, g):
    return x * lax.rsqrt(jnp.mean(x * x, axis=-1, keepdims=True) + EPS) * g


def _softplus_neg_abs(z):
    return jnp.log1p(jnp.exp(-jnp.abs(z)))


def _log_sigmoid(z):
    return jnp.minimum(z, 0.0) - _softplus_neg_abs(z)


def _silu(x):
    return x * jax.nn.sigmoid(x)


def _head0_lanes(shape):
    return lax.broadcasted_iota(jnp.int32, shape, len(shape) - 1) < HEAD_DIM


def _resident(shape):
    nd = len(shape)
    return pl.BlockSpec(shape, lambda *_: (0,) * nd, pipeline_mode=pl.Buffered(1))


def _params(*sem):
    return pltpu.CompilerParams(dimension_semantics=sem, vmem_limit_bytes=VMEM_LIMIT)


def _rope(x, cos, sin_lo, sin_hi):
    half = ROPE_DIMS // 2
    return x * cos + pltpu.roll(x, LANES - half, 1) * sin_lo + pltpu.roll(x, half, 1) * sin_hi


def _inproj0_kernel(x_ref, g_ref, w_ref, wa2_ref, ba_ref, cos_ref, sinlo_ref, sinhi_ref,
                    aq_ref, ak_ref, av_ref, ag_ref, la_ref, bq_ref, bk_ref, bv_ref):
    xn = _rmsnorm(x_ref[...], g_ref[...]).astype(BF16)

    def proj(lo, hi):
        return _dot(xn, w_ref[:, lo:hi])

    aq_ref[...] = (proj(0, 256) * (GLA_DK ** -0.5)).astype(BF16)
    ak_ref[...] = proj(256, 512).astype(BF16)
    av_ref[...] = proj(512, 1024).astype(BF16)
    ag_ref[...] = proj(1024, 1536)
    ar = proj(3072, 3200)
    gate_logit = _dot(ar.astype(BF16), wa2_ref[...]) + ba_ref[...]
    la_ref[...] = _log_sigmoid(gate_logit) / GLA_TAU
    cos, sin_lo, sin_hi = cos_ref[...], sinlo_ref[...], sinhi_ref[...]
    for blk in range(4):
        lo = 1536 + blk * LANES
        sl = slice(blk * LANES, (blk + 1) * LANES)
        bq_ref[:, sl] = _rope(proj(lo, lo + LANES), cos, sin_lo, sin_hi) * (HEAD_DIM ** -0.5)
        bk_ref[:, sl] = _rope(proj(lo + 512, lo + 512 + LANES), cos, sin_lo, sin_hi)
    bv_ref[...] = proj(2560, 3072)


def _rope_tables(seq):
    half = ROPE_DIMS // 2
    inv = ROPE_THETA ** (-jnp.arange(half, dtype=F32) / half)
    ang = jnp.arange(seq, dtype=F32)[:, None] * inv[None, :]
    cos, sin = jnp.cos(ang), jnp.sin(ang)
    ones = jnp.ones((seq, HEAD_DIM - ROPE_DIMS), F32)
    zeros = jnp.zeros((seq, HEAD_DIM - ROPE_DIMS), F32)
    zh = jnp.zeros((seq, half), F32)
    cos_h = jnp.concatenate([cos, cos, ones], axis=1)
    lo_h = jnp.concatenate([-sin, zh, zeros], axis=1)
    hi_h = jnp.concatenate([zh, sin, zeros], axis=1)
    two = lambda t: jnp.concatenate([t, t], axis=1)
    return two(cos_h), two(lo_h), two(hi_h)


def _inproj0(x2, g, w_in, wa2, ba, seq):
    T = x2.shape[0]
    tm = PROJ_TM
    aw_k, aw_v, bw = GLA_HEADS * GLA_DK, GLA_HEADS * GLA_DV, DSW_HEADS * HEAD_DIM
    o = [0, aw_k, 2 * aw_k, 2 * aw_k + aw_v, 2 * aw_k + 2 * aw_v]
    o.append(o[-1] + GLA_RANK)
    ar_cols = w_in[:, o[4]:o[5]]
    w = jnp.concatenate(
        [w_in[:, :o[4]], w_in[:, o[5]:], ar_cols, jnp.zeros((D_MODEL, LANES - GLA_RANK), F32)], axis=1
    ).astype(BF16)
    wa2p = jnp.concatenate([wa2, jnp.zeros((LANES - GLA_RANK, aw_k), F32)], axis=0).astype(BF16)
    cos, sin_lo, sin_hi = _rope_tables(seq)
    spt = seq // tm
    row = lambda n: pl.BlockSpec((tm, n), lambda i: (i, 0))
    tab = pl.BlockSpec((tm, LANES), lambda i: (i % spt, 0))
    outs = [(aw_k, BF16), (aw_k, BF16), (aw_v, BF16), (aw_v, F32), (aw_k, F32), (bw, F32), (bw, F32), (bw, F32)]
    return pl.pallas_call(
        _inproj0_kernel,
        grid=(T // tm,),
        in_specs=[row(D_MODEL), _resident((1, D_MODEL)), _resident(w.shape), _resident(wa2p.shape),
                  _resident((1, aw_k)), tab, tab, tab],
        out_specs=[row(n) for n, _ in outs],
        out_shape=[jax.ShapeDtypeStruct((T, n), dt) for n, dt in outs],
        compiler_params=_params("parallel"),
        name="inproj0",
    )(x2, g.reshape(1, -1), w, wa2p, ba.reshape(1, -1), cos, sin_lo, sin_hi)


def _gla_kernel(q_ref, k_ref, la_ref, v_ref, gate_ref, gn_ref, o_ref, st0_ref, st1_ref):
    C = GLA_CHUNK
    seq = q_ref.shape[0]
    st_refs = (st0_ref, st1_ref)
    for st in st_refs:
        st[...] = jnp.zeros_like(st)
    head0 = _head0_lanes((C, LANES))
    r = lax.broadcasted_iota(jnp.int32, (C, C), 0)
    c = lax.broadcasted_iota(jnp.int32, (C, C), 1)
    tril = c <= r
    tri = jnp.where(tril, 1.0, 0.0).astype(BF16)
    gn = gn_ref[...]

    def body(n, _):
        r0 = pl.multiple_of(n * C, C)
        rows = pl.ds(r0, C)
        la = la_ref[rows, :]
        hi = la.astype(BF16)
        rem = la - hi.astype(F32)
        mid = rem.astype(BF16)
        lo = (rem - mid.astype(F32)).astype(BF16)
        bcum = _dot(tri, hi) + _dot(tri, mid) + _dot(tri, lo)
        btot = bcum[C - 1:C, :]
        q = q_ref[rows, :].astype(F32)
        k = k_ref[rows, :].astype(F32)
        q_dec = q * jnp.exp(bcum)
        k_inv = (k * jnp.exp(-bcum)).astype(BF16)
        k_tail = k * jnp.exp(btot - bcum)
        dec = jnp.exp(btot)
        for h in range(2):
            lanes_h = head0 if h == 0 else jnp.logical_not(head0)
            cols = slice(h * GLA_DV, (h + 1) * GLA_DV)
            qh = jnp.where(lanes_h, q_dec, 0.0).astype(BF16)
            kth = jnp.where(lanes_h, k_tail, 0.0).astype(BF16)
            vh = v_ref[rows, cols]
            scores = jnp.where(tril, _dot_nt(qh, k_inv), 0.0)
            st = st_refs[h][...]
            o = _dot(scores.astype(BF16), vh) + _dot_nt(qh, st.astype(BF16))
            st_refs[h][...] = st * dec + _dot_tn(vh, kth)
            y = _rmsnorm(o, gn) * _silu(gate_ref[rows, cols])
            o_ref[rows, cols] = y.astype(o_ref.dtype)
        return 0

    lax.fori_loop(0, seq // C, body, 0)


def _gla(aq, ak, la, av, ag, gla_norm, batch, seq):
    npairs = GLA_HEADS // 2
    qk = pl.BlockSpec((seq, LANES), lambda b, p: (b, p))
    vv = pl.BlockSpec((seq, 2 * GLA_DV), lambda b, p: (b, p))
    return pl.pallas_call(
        _gla_kernel,
        grid=(batch, npairs),
        in_specs=[qk, qk, qk, vv, vv, _resident((1, GLA_DV))],
        out_specs=vv,
        out_shape=jax.ShapeDtypeStruct((batch * seq, GLA_HEADS * GLA_DV), BF16),
        scratch_shapes=[pltpu.VMEM((GLA_DV, LANES), F32), pltpu.VMEM((GLA_DV, LANES), F32)],
        compiler_params=_params("parallel", "parallel"),
        name="gla",
    )(aq, ak, la, av, ag, gla_norm.reshape(1, -1))


def _dsw_kernel(q_ref, k_ref, v_ref, o_ref, m0_ref, m1_ref, l0_ref, l1_ref, a0_ref, a1_ref):
    C = DSW_BLOCK
    seq = q_ref.shape[0]
    m_refs, l_refs, a_refs = (m0_ref, m1_ref), (l0_ref, l1_ref), (a0_ref, a1_ref)
    head0 = _head0_lanes((C, LANES))
    qi = lax.broadcasted_iota(jnp.int32, (C, 2 * C), 0)
    ki = lax.broadcasted_iota(jnp.int32, (C, 2 * C), 1)
    is_prev = ki < C
    valid_cur = jnp.logical_and(ki >= C, ki - C <= qi)

    def rows(start, d):
        return pl.ds(start, C) if d == 1 else pl.ds(start, C, stride=d)

    for pat, (window, d) in enumerate(DSW_PATTERNS):
        assert window // d == C
        nb = seq // d // C

        def body(idx, _, d=d, nb=nb, first=(pat == 0)):
            stream = idx // nb
            blk = idx % nb
            start = stream + d * C * blk
            cur = rows(start, d)
            q = q_ref[cur, :]
            if nb > 1:
                prev = rows(jnp.maximum(start - d * C, 0), d)
                kcat = jnp.concatenate([k_ref[prev, :], k_ref[cur, :]], axis=0).astype(BF16)
                vcat = jnp.concatenate([v_ref[prev, :], v_ref[cur, :]], axis=0).astype(BF16)
                first_key = qi + jnp.where(blk > 0, 0, 2 * C)
                valid = jnp.logical_or(jnp.logical_and(is_prev, ki >= first_key), valid_cur)
            else:
                kcat = k_ref[cur, :].astype(BF16)
                vcat = v_ref[cur, :].astype(BF16)
                valid = valid_cur[:, C:]
            for h in range(2):
                lanes_h = head0 if h == 0 else jnp.logical_not(head0)
                qh = jnp.where(lanes_h, q, 0.0).astype(BF16)
                s = jnp.where(valid, _dot_nt(qh, kcat), MASK_NEG)
                m_blk = jnp.max(s, axis=-1, keepdims=True)
                if first:
                    m_new = m_blk
                else:
                    m_old = m_refs[h][cur, :][:, 0:1]
                    m_new = jnp.maximum(m_old, m_blk)
                p = jnp.exp(s - m_new)
                den = jnp.sum(p, axis=-1, keepdims=True)
                num = _dot(p.astype(BF16), vcat)
                if not first:
                    alpha = jnp.exp(m_old - m_new)
                    den = den + alpha * l_refs[h][cur, :][:, 0:1]
                    num = num + alpha * a_refs[h][cur, :]
                m_refs[h][cur, :] = jnp.broadcast_to(m_new, (C, LANES))
                l_refs[h][cur, :] = jnp.broadcast_to(den, (C, LANES))
                a_refs[h][cur, :] = num
            return 0

        lax.fori_loop(0, seq // C, body, 0)

    full0 = _head0_lanes((seq, LANES))
    out = jnp.where(full0, a0_ref[...] / l0_ref[...], a1_ref[...] / l1_ref[...])
    o_ref[...] = out.astype(o_ref.dtype)


def _dsw(bq, bk, bv, batch, seq):
    npairs = DSW_HEADS // 2
    spec = pl.BlockSpec((seq, LANES), lambda b, p: (b, p))
    return pl.pallas_call(
        _dsw_kernel,
        grid=(batch, npairs),
        in_specs=[spec, spec, spec],
        out_specs=spec,
        out_shape=jax.ShapeDtypeStruct((batch * seq, DSW_HEADS * HEAD_DIM), BF16),
        scratch_shapes=[pltpu.VMEM((seq, LANES), F32) for _ in range(6)],
        compiler_params=_params("parallel", "parallel"),
        name="dsw",
    )(bq, bk, bv)


def _ffn_kernel(h_ref, ma_ref, mb_ref, wo_ref, gf_ref, wg_ref, wv_ref, cg_ref, cv_ref, wd_ref, gfin_ref,
                o_ref, carry_ref, ubuf_ref, xn_ref, acc_ref, *, tiles_per_seq, final_norm):
    tm = h_ref.shape[0]
    half = ma_ref.shape[1]
    nchunks = wg_ref.shape[0]
    H = SUBLANES

    @pl.when(pl.program_id(0) % tiles_per_seq == 0)
    def _():
        carry_ref[...] = jnp.zeros_like(carry_ref)

    h1 = h_ref[...] + _dot(ma_ref[...], wo_ref[0:half, :]) + _dot(mb_ref[...], wo_ref[half:, :])
    xn_ref[...] = _rmsnorm(h1, gf_ref[...]).astype(BF16)
    acc_ref[...] = h1

    def conv(u, w, slot, c):
        buf = ubuf_ref.at[slot]
        buf[0:H, :] = carry_ref[c, slot]
        buf[H:, :] = u
        carry_ref[c, slot] = buf[tm:tm + H, :]
        out = w[FFN_CONV - 1:FFN_CONV, :] * u
        for j in range(1, FFN_CONV):
            out = out + w[FFN_CONV - 1 - j:FFN_CONV - j, :] * buf[pl.ds(H - j, tm), :]
        return out

    def body(c, _):
        xn = xn_ref[...]
        g = conv(_dot(xn, wg_ref[c]), cg_ref[c], 0, c)
        v = conv(_dot(xn, wv_ref[c]), cv_ref[c], 1, c)
        act = (_silu(g) * v).astype(BF16)
        acc_ref[...] += _dot(act, wd_ref[c])
        return 0

    lax.fori_loop(0, nchunks, body, 0)
    out = acc_ref[...]
    if final_norm:
        out = _rmsnorm(out, gfin_ref[...])
    o_ref[...] = out


def _outproj_ffn(h2, mix_a, mix_b, w_out, g_ffn, w_up, w_conv, w_down, g_final, seq, final_norm):
    T = h2.shape[0]
    tm, tf = FFN_TM, FFN_TF
    nchunks = D_FF // tf
    half = mix_a.shape[1]
    chunked_cols = lambda w: w.reshape(w.shape[0], nchunks, tf).transpose(1, 0, 2)
    wg = chunked_cols(w_up[:, :D_FF]).astype(BF16)
    wv = chunked_cols(w_up[:, D_FF:]).astype(BF16)
    pad = jnp.zeros((SUBLANES - FFN_CONV, 2 * D_FF), F32)
    wc = jnp.concatenate([w_conv, pad], axis=0)
    cg = chunked_cols(wc[:, :D_FF])
    cv = chunked_cols(wc[:, D_FF:])
    wd = w_down.reshape(nchunks, tf, D_MODEL).astype(BF16)
    row = lambda n: pl.BlockSpec((tm, n), lambda i: (i, 0))
    kern = functools.partial(_ffn_kernel, tiles_per_seq=seq // tm, final_norm=final_norm)
    return pl.pallas_call(
        kern,
        grid=(T // tm,),
        in_specs=[row(D_MODEL), row(half), row(half), _resident((2 * half, D_MODEL)), _resident((1, D_MODEL)),
                  _resident(wg.shape), _resident(wv.shape), _resident(cg.shape), _resident(cv.shape),
                  _resident(wd.shape), _resident((1, D_MODEL))],
        out_specs=row(D_MODEL),
        out_shape=jax.ShapeDtypeStruct((T, D_MODEL), F32),
        scratch_shapes=[pltpu.VMEM((nchunks, 2, SUBLANES, tf), F32),
                        pltpu.VMEM((2, tm + SUBLANES, tf), F32),
                        pltpu.VMEM((tm, D_MODEL), BF16),
                        pltpu.VMEM((tm, D_MODEL), F32)],
        compiler_params=_params("arbitrary"),
        name="outproj_ffn",
    )(h2, mix_a, mix_b, w_out.astype(BF16), g_ffn.reshape(1, -1), wg, wv, cg, cv, wd, g_final.reshape(1, -1))


def _inproj1_kernel(x_ref, g_ref, w_ref, c_ref, q_ref, k_ref, v_ref):
    xn = _rmsnorm(x_ref[...], g_ref[...]).astype(BF16)

    def proj(lo, hi):
        return _dot(xn, w_ref[:, lo:hi])

    c_ref[...] = proj(0, 512) * jax.nn.sigmoid(proj(512, 1024))
    q_ref[...] = (proj(1024, 1536) * (HEAD_DIM ** -0.5)).astype(BF16)
    k_ref[...] = proj(1536, 2048).astype(BF16)
    v_ref[...] = proj(2048, 2560).astype(BF16)


def _inproj1(x2, g, w_in):
    T = x2.shape[0]
    tm = PROJ_TM
    row = lambda n: pl.BlockSpec((tm, n), lambda i: (i, 0))
    outs = [(CONV_CH, F32), (512, BF16), (512, BF16), (512, BF16)]
    return pl.pallas_call(
        _inproj1_kernel,
        grid=(T // tm,),
        in_specs=[row(D_MODEL), _resident((1, D_MODEL)), _resident(w_in.shape)],
        out_specs=[row(n) for n, _ in outs],
        out_shape=[jax.ShapeDtypeStruct((T, n), dt) for n, dt in outs],
        compiler_params=_params("parallel"),
        name="inproj1",
    )(x2, g.reshape(1, -1), w_in.astype(BF16))


def _convmod_kernel(c_ref, w_ref, b_ref, lg_ref, lb_ref, o_ref, buf_ref):
    ts = c_ref.shape[0]
    HALO = CONV_HALO

    @pl.when(pl.program_id(1) == 0)
    def _():
        buf_ref[0:HALO, :] = jnp.zeros((HALO, CONV_CH), F32)

    @pl.when(pl.program_id(1) > 0)
    def _():
        buf_ref[0:HALO, :] = buf_ref[ts:ts + HALO, :]

    buf_ref[HALO:, :] = c_ref[...]
    bias, lg, lb = b_ref[...], lg_ref[...], lb_ref[...]
    R = CONV_RB
    for rb in range(ts // R):
        acc = jnp.zeros((R, CONV_CH), F32)
        for kk in range(CONV_WIDTH):
            off = rb * R + HALO - (CONV_WIDTH - 1) + kk
            acc = acc + w_ref[kk:kk + 1, :] * buf_ref[off:off + R, :]
        y = acc + bias
        mu = jnp.mean(y, axis=-1, keepdims=True)
        yc = y - mu
        var = jnp.mean(yc * yc, axis=-1, keepdims=True)
        z = yc * lax.rsqrt(var + EPS) * lg + lb
        o_ref[rb * R:(rb + 1) * R, :] = _silu(z).astype(o_ref.dtype)


def _convmod(c, conv_w, conv_b, ln_g, ln_b, batch, seq):
    ts = CONV_TS
    spb = seq // ts
    wpad = jnp.concatenate([conv_w, jnp.zeros((32 - CONV_WIDTH, CONV_CH), F32)], axis=0)
    spec = pl.BlockSpec((ts, CONV_CH), lambda b, s: (b * spb + s, 0))
    vec = _resident((1, CONV_CH))
    return pl.pallas_call(
        _convmod_kernel,
        grid=(batch, spb),
        in_specs=[spec, _resident(wpad.shape), vec, vec, vec],
        out_specs=spec,
        out_shape=jax.ShapeDtypeStruct((batch * seq, CONV_CH), BF16),
        scratch_shapes=[pltpu.VMEM((CONV_HALO + ts, CONV_CH), F32)],
        compiler_params=_params("parallel", "arbitrary"),
        name="convmod",
    )(c, wpad, conv_b.reshape(1, -1), ln_g.reshape(1, -1), ln_b.reshape(1, -1))


def _sb_kernel(q_ref, k_ref, v_ref, u_ref, o_ref, carry_ref, acc0_ref, acc1_ref):
    TQ, TK = SB_TQ, SB_TK
    qblk = pl.program_id(2)
    q = q_ref[...]
    head0 = _head0_lanes((TQ, LANES))
    qpos = lax.broadcasted_iota(jnp.int32, (TQ, TK), 0)
    kpos = lax.broadcasted_iota(jnp.int32, (TQ, TK), 1)
    below_diag = kpos < qpos
    u = u_ref[...]
    acc_refs = (acc0_ref, acc1_ref)

    for h in range(2):
        lanes_h = head0 if h == 0 else jnp.logical_not(head0)
        qh = jnp.where(lanes_h, q, jnp.zeros_like(q))
        acc_ref = acc_refs[h]
        acc_ref[...] = jnp.zeros_like(acc_ref)
        carry_ref[...] = jnp.zeros_like(carry_ref)

        def block(kb, masked, qh=qh, acc_ref=acc_ref):
            rows = pl.ds(pl.multiple_of(kb * TK, TK), TK)
            z = _dot_nt(qh, k_ref[rows, :])
            sp = _softplus_neg_abs(z)
            log_beta = jnp.minimum(z, 0.0) - sp
            log_1m = -jnp.maximum(z, 0.0) - sp
            if masked:
                log_1m = jnp.where(below_diag, log_1m, 0.0)
            hi = log_1m.astype(BF16)
            lo = (log_1m - hi.astype(F32)).astype(BF16)
            incl = _dot(hi, u) + _dot(lo, u)
            carry = carry_ref[...]
            after = incl - log_1m + jnp.concatenate([carry] * (TK // LANES), axis=1)
            a = jnp.exp(log_beta + after)
            if masked:
                a = jnp.where(below_diag, a, 0.0)
            acc_ref[...] += _dot(a.astype(BF16), v_ref[rows, :])
            carry_ref[...] = carry + jnp.broadcast_to(incl[:, 0:1], (TQ, LANES))

        block(qblk, True)

        def body(i, _, block=block):
            block(qblk - 1 - i, False)
            return 0

        lax.fori_loop(0, qblk, body, 0)

    o_ref[...] = jnp.where(head0, acc0_ref[...], acc1_ref[...]).astype(o_ref.dtype)


def _sb(dq, dk, dv, batch, seq):
    TQ, TK = SB_TQ, SB_TK
    npairs = SB_HEADS // 2
    nq = seq // TQ
    j = lax.broadcasted_iota(jnp.int32, (TK, TK), 0)
    s = lax.broadcasted_iota(jnp.int32, (TK, TK), 1)
    u = jnp.where(j >= s, 1.0, 0.0).astype(BF16)
    qspec = pl.BlockSpec((TQ, LANES), lambda b, p, i: (b * nq + i, p))
    kvspec = pl.BlockSpec((seq, LANES), lambda b, p, i: (b, p))
    return pl.pallas_call(
        _sb_kernel,
        grid=(batch, npairs, nq),
        in_specs=[qspec, kvspec, kvspec, _resident((TK, TK))],
        out_specs=qspec,
        out_shape=jax.ShapeDtypeStruct((batch * seq, SB_HEADS * HEAD_DIM), BF16),
        scratch_shapes=[pltpu.VMEM((TQ, LANES), F32) for _ in range(3)],
        compiler_params=_params("parallel", "parallel", "arbitrary"),
        name="stickbreak",
    )(dq, dk, dv, u)


def kernel(x, norm_mix0, w_in0, gla_wa2, gla_ba, gla_norm, w_out0, norm_ffn0, ffn_up0, ffn_conv0, ffn_down0,
           norm_mix1, w_in1, conv_w1, conv_b1, conv_ln_g1, conv_ln_b1, w_out1, norm_ffn1, ffn_up1, ffn_conv1,
           ffn_down1, final_norm):
    batch, seq, d = x.shape
    h = x.reshape(batch * seq, d)
    aq, ak, av, ag, la, bq, bk, bv = _inproj0(h, norm_mix0, w_in0, gla_wa2, gla_ba, seq)
    oa = _gla(aq, ak, la, av, ag, gla_norm, batch, seq)
    ob = _dsw(bq, bk, bv, batch, seq)
    h = _outproj_ffn(h, oa, ob, w_out0, norm_ffn0, ffn_up0, ffn_conv0, ffn_down0, final_norm, seq, False)
    c, dq, dk, dv = _inproj1(h, norm_mix1, w_in1)
    oc = _convmod(c, conv_w1, conv_b1, conv_ln_g1, conv_ln_b1, batch, seq)
    od = _sb(dq, dk, dv, batch, seq)
    h = _outproj_ffn(h, oc, od, w_out1, norm_ffn1, ffn_up1, ffn_conv1, ffn_down1, final_norm, seq, True)
    return h.reshape(batch, seq, d)
```

```python
import functools

import jax
import jax.numpy as jnp
from jax import lax
from jax.experimental import pallas as pl
from jax.experimental.pallas import tpu as pltpu

F32 = jnp.float32
BF16 = jnp.bfloat16

D_MODEL = 1024
HEAD_DIM = 64
EPS = 1e-6
GLA_HEADS = 4
GLA_DV = 128
GLA_DK = 64
GLA_RANK = 16
GLA_TAU = 16.0
GLA_CHUNK = 64
DSW_HEADS = 8
DSW_PATTERNS = ((128, 1), (512, 4), (2048, 16))
DSW_BLOCK = 128
CONV_CH = 512
CONV_WIDTH = 31
SB_HEADS = 8
D_FF = 2816
FFN_CONV = 3
ROPE_THETA = 500000.0
ROPE_DIMS = 16

LANES = 128
SUBLANES = 8
VMEM_LIMIT = 56 * 1024 * 1024
MASK_NEG = -1e30
SIGN_BIT = 0x80000000

PROJ_TM = 512
FFN_TM = 512
FFN_TF = 256
CONV_TS = 256
CONV_HALO = 32
CONV_RB = 32
SB_TQ = 256
SB_TK = 256
SB_NSUB = 4
DSW_GROUP = 4
DSW_COMBINE_ROWS = 256


def _dot(a, b):
    return jnp.dot(a, b, preferred_element_type=F32)


def _dot_nt(a, b):
    return lax.dot_general(a, b, (((1,), (1,)), ((), ())), preferred_element_type=F32)


def _dot_tn(a, b):
    return lax.dot_general(a, b, (((0,), (0,)), ((), ())), preferred_element_type=F32)


def _rmsnorm(x, g):
    return x * lax.rsqrt(jnp.mean(x * x, axis=-1, keepdims=True) + EPS) * g


def _softplus_neg_abs(z):
    return jnp.log1p(jnp.exp(-jnp.abs(z)))


def _log_sigmoid(z):
    return jnp.minimum(z, 0.0) - _softplus_neg_abs(z)


def _silu(x):
    return x * jax.nn.sigmoid(x)


def _head0_lanes(shape):
    return lax.broadcasted_iota(jnp.int32, shape, len(shape) - 1) < HEAD_DIM


def _resident(shape):
    nd = len(shape)
    return pl.BlockSpec(shape, lambda *_: (0,) * nd, pipeline_mode=pl.Buffered(1))


def _params(*sem):
    return pltpu.CompilerParams(dimension_semantics=sem, vmem_limit_bytes=VMEM_LIMIT)


def _rope(x, cos, sin_lo, sin_hi):
    half = ROPE_DIMS // 2
    return x * cos + pltpu.roll(x, LANES - half, 1) * sin_lo + pltpu.roll(x, half, 1) * sin_hi


def _inproj0_kernel(x_ref, g_ref, w_ref, wa2_ref, ba_ref, cos_ref, sinlo_ref, sinhi_ref,
                    aq_ref, ak_ref, av_ref, ag_ref, la_ref, bq_ref, bk_ref, bv_ref):
    xn = _rmsnorm(x_ref[...], g_ref[...]).astype(BF16)

    def proj(lo, hi):
        return _dot(xn, w_ref[:, lo:hi])

    aq_ref[...] = (proj(0, 256) * (GLA_DK ** -0.5)).astype(BF16)
    ak_ref[...] = proj(256, 512).astype(BF16)
    av_ref[...] = proj(512, 1024).astype(BF16)
    ag_ref[...] = proj(1024, 1536)
    ar = proj(3072, 3200)
    gate_logit = _dot(ar.astype(BF16), wa2_ref[...]) + ba_ref[...]
    la_ref[...] = _log_sigmoid(gate_logit) / GLA_TAU
    cos, sin_lo, sin_hi = cos_ref[...], sinlo_ref[...], sinhi_ref[...]
    for blk in range(4):
        lo = 1536 + blk * LANES
        sl = slice(blk * LANES, (blk + 1) * LANES)
        bq_ref[:, sl] = _rope(proj(lo, lo + LANES), cos, sin_lo, sin_hi) * (HEAD_DIM ** -0.5)
        bk_ref[:, sl] = _rope(proj(lo + 512, lo + 512 + LANES), cos, sin_lo, sin_hi)
    bv_ref[...] = proj(2560, 3072)


def _rope_tables(seq):
    half = ROPE_DIMS // 2
    inv = ROPE_THETA ** (-jnp.arange(half, dtype=F32) / half)
    ang = jnp.arange(seq, dtype=F32)[:, None] * inv[None, :]
    cos, sin = jnp.cos(ang), jnp.sin(ang)
    ones = jnp.ones((seq, HEAD_DIM - ROPE_DIMS), F32)
    zeros = jnp.zeros((seq, HEAD_DIM - ROPE_DIMS), F32)
    zh = jnp.zeros((seq, half), F32)
    cos_h = jnp.concatenate([cos, cos, ones], axis=1)
    lo_h = jnp.concatenate([-sin, zh, zeros], axis=1)
    hi_h = jnp.concatenate([zh, sin, zeros], axis=1)
    two = lambda t: jnp.concatenate([t, t], axis=1)
    return two(cos_h), two(lo_h), two(hi_h)


def _inproj0(x2, g, w_in, wa2, ba, seq):
    T = x2.shape[0]
    tm = PROJ_TM
    aw_k, aw_v, bw = GLA_HEADS * GLA_DK, GLA_HEADS * GLA_DV, DSW_HEADS * HEAD_DIM
    o = [0, aw_k, 2 * aw_k, 2 * aw_k + aw_v, 2 * aw_k + 2 * aw_v]
    o.append(o[-1] + GLA_RANK)
    ar_cols = w_in[:, o[4]:o[5]]
    w = jnp.concatenate(
        [w_in[:, :o[4]], w_in[:, o[5]:], ar_cols, jnp.zeros((D_MODEL, LANES - GLA_RANK), F32)], axis=1
    ).astype(BF16)
    wa2p = jnp.concatenate([wa2, jnp.zeros((LANES - GLA_RANK, aw_k), F32)], axis=0).astype(BF16)
    cos, sin_lo, sin_hi = _rope_tables(seq)
    spt = seq // tm
    row = lambda n: pl.BlockSpec((tm, n), lambda i: (i, 0))
    tab = pl.BlockSpec((tm, LANES), lambda i: (i % spt, 0))
    outs = [(aw_k, BF16), (aw_k, BF16), (aw_v, BF16), (aw_v, F32), (aw_k, F32), (bw, F32), (bw, F32), (bw, F32)]
    return pl.pallas_call(
        _inproj0_kernel,
        grid=(T // tm,),
        in_specs=[row(D_MODEL), _resident((1, D_MODEL)), _resident(w.shape), _resident(wa2p.shape),
                  _resident((1, aw_k)), tab, tab, tab],
        out_specs=[row(n) for n, _ in outs],
        out_shape=[jax.ShapeDtypeStruct((T, n), dt) for n, dt in outs],
        compiler_params=_params("parallel"),
        name="inproj0",
    )(x2, g.reshape(1, -1), w, wa2p, ba.reshape(1, -1), cos, sin_lo, sin_hi)


def _gla_kernel(q_ref, k_ref, la_ref, v_ref, gate_ref, gn_ref, o_ref, st0_ref, st1_ref):
    C = GLA_CHUNK
    seq = q_ref.shape[0]
    st_refs = (st0_ref, st1_ref)
    for st in st_refs:
        st[...] = jnp.zeros_like(st)
    head0 = _head0_lanes((C, LANES))
    r = lax.broadcasted_iota(jnp.int32, (C, C), 0)
    c = lax.broadcasted_iota(jnp.int32, (C, C), 1)
    tril = c <= r
    tri = jnp.where(tril, 1.0, 0.0).astype(BF16)
    gn = gn_ref[...]

    def body(n, _):
        r0 = pl.multiple_of(n * C, C)
        rows = pl.ds(r0, C)
        la = la_ref[rows, :]
        hi = la.astype(BF16)
        rem = la - hi.astype(F32)
        mid = rem.astype(BF16)
        lo = (rem - mid.astype(F32)).astype(BF16)
        bcum = _dot(tri, hi) + _dot(tri, mid) + _dot(tri, lo)
        btot = bcum[C - 1:C, :]
        q = q_ref[rows, :].astype(F32)
        k = k_ref[rows, :].astype(F32)
        q_dec = q * jnp.exp(bcum)
        k_inv = (k * jnp.exp(-bcum)).astype(BF16)
        k_tail = k * jnp.exp(btot - bcum)
        dec = jnp.exp(btot)
        for h in range(2):
            lanes_h = head0 if h == 0 else jnp.logical_not(head0)
            cols = slice(h * GLA_DV, (h + 1) * GLA_DV)
            qh = jnp.where(lanes_h, q_dec, 0.0).astype(BF16)
            kth = jnp.where(lanes_h, k_tail, 0.0).astype(BF16)
            vh = v_ref[rows, cols]
            scores = jnp.where(tril, _dot_nt(qh, k_inv), 0.0)
            st = st_refs[h][...]
            o = _dot(scores.astype(BF16), vh) + _dot_nt(qh, st.astype(BF16))
            st_refs[h][...] = st * dec + _dot_tn(vh, kth)
            y = _rmsnorm(o, gn) * _silu(gate_ref[rows, cols])
            o_ref[rows, cols] = y.astype(o_ref.dtype)
        return 0

    lax.fori_loop(0, seq // C, body, 0)


def _gla(aq, ak, la, av, ag, gla_norm, batch, seq):
    npairs = GLA_HEADS // 2
    qk = pl.BlockSpec((seq, LANES), lambda b, p: (b, p))
    vv = pl.BlockSpec((seq, 2 * GLA_DV), lambda b, p: (b, p))
    return pl.pallas_call(
        _gla_kernel,
        grid=(batch, npairs),
        in_specs=[qk, qk, qk, vv, vv, _resident((1, GLA_DV))],
        out_specs=vv,
        out_shape=jax.ShapeDtypeStruct((batch * seq, GLA_HEADS * GLA_DV), BF16),
        scratch_shapes=[pltpu.VMEM((GLA_DV, LANES), F32), pltpu.VMEM((GLA_DV, LANES), F32)],
        compiler_params=_params("parallel", "parallel"),
        name="gla",
    )(aq, ak, la, av, ag, gla_norm.reshape(1, -1))


def _dsw_kernel(q_ref, k_ref, v_ref, o_ref, *stat_refs):
    C = DSW_BLOCK
    seq = q_ref.shape[0]
    npat = len(DSW_PATTERNS)
    m_refs, l_refs, a_refs = stat_refs[0:npat], stat_refs[npat:2 * npat], stat_refs[2 * npat:3 * npat]
    head0 = _head0_lanes((C, LANES))
    qi = lax.broadcasted_iota(jnp.int32, (C, 2 * C), 0)
    ki = lax.broadcasted_iota(jnp.int32, (C, 2 * C), 1)
    is_prev = ki < C
    valid_cur = jnp.logical_and(ki >= C, ki - C <= qi)

    def rows(start, d):
        return pl.ds(start, C) if d == 1 else pl.ds(start, C, stride=d)

    for pat, (window, d) in enumerate(DSW_PATTERNS):
        assert window // d == C
        nb = seq // d // C

        def body(it, _, d=d, nb=nb, pat=pat):
            G = DSW_GROUP
            base = it * G
            if nb > 1:
                assert nb % G == 0
                stream, blk0 = base // nb, base % nb
                starts = [stream + d * C * (blk0 + j) for j in range(G)]
                first = rows(jnp.maximum(starts[0] - d * C, 0), d)
                kb = [k_ref[first, :].astype(BF16)] + [k_ref[rows(s, d), :].astype(BF16) for s in starts]
                vb = [v_ref[first, :].astype(BF16)] + [v_ref[rows(s, d), :].astype(BF16) for s in starts]
                kcats = [jnp.concatenate([kb[j], kb[j + 1]], axis=0) for j in range(G)]
                vcats = [jnp.concatenate([vb[j], vb[j + 1]], axis=0) for j in range(G)]
                first_key = qi + jnp.where(blk0 > 0, 0, 2 * C)
                valid0 = jnp.logical_or(jnp.logical_and(is_prev, ki >= first_key), valid_cur)
                valids = [valid0] + [jnp.logical_or(jnp.logical_and(is_prev, ki >= qi), valid_cur)] * (G - 1)
            else:
                starts = [base + j for j in range(G)]
                kcats = [k_ref[rows(s, d), :].astype(BF16) for s in starts]
                vcats = [v_ref[rows(s, d), :].astype(BF16) for s in starts]
                valids = [valid_cur[:, C:]] * G
            scores = []
            for j in range(G):
                q = q_ref[rows(starts[j], d), :]
                for h in range(2):
                    lanes_h = head0 if h == 0 else jnp.logical_not(head0)
                    scores.append(_dot_nt(jnp.where(lanes_h, q, 0.0).astype(BF16), kcats[j]))
            stats = []
            for i, s in enumerate(scores):
                s = jnp.where(valids[i // 2], s, MASK_NEG)
                m = jnp.max(s, axis=-1, keepdims=True)
                p = jnp.exp(s - m)
                stats.append((m, jnp.sum(p, axis=-1, keepdims=True), p.astype(BF16)))
            nums = [_dot(p, vcats[i // 2]) for i, (_, _, p) in enumerate(stats)]
            for j in range(G):
                cur = rows(starts[j], d)
                (m0, den0, _), (m1, den1, _) = stats[2 * j], stats[2 * j + 1]
                m_refs[pat][cur, :] = jnp.where(head0, m0, m1)
                l_refs[pat][cur, :] = jnp.where(head0, den0, den1)
                a_refs[pat][cur, :] = jnp.where(head0, nums[2 * j], nums[2 * j + 1])
            return 0

        lax.fori_loop(0, seq // C // DSW_GROUP, body, 0)

    R = DSW_COMBINE_ROWS
    for r0 in range(0, seq, R):
        rows_r = slice(r0, r0 + R)
        ms = [m[rows_r, :] for m in m_refs]
        m_max = functools.reduce(jnp.maximum, ms)
        ws = [jnp.exp(m - m_max) for m in ms]
        num = sum(w * a[rows_r, :] for w, a in zip(ws, a_refs))
        den = sum(w * l[rows_r, :] for w, l in zip(ws, l_refs))
        o_ref[rows_r, :] = (num / den).astype(o_ref.dtype)


def _dsw(bq, bk, bv, batch, seq):
    npairs = DSW_HEADS // 2
    spec = pl.BlockSpec((seq, LANES), lambda b, p: (b, p))
    return pl.pallas_call(
        _dsw_kernel,
        grid=(batch, npairs),
        in_specs=[spec, spec, spec],
        out_specs=spec,
        out_shape=jax.ShapeDtypeStruct((batch * seq, DSW_HEADS * HEAD_DIM), BF16),
        scratch_shapes=[pltpu.VMEM((seq, LANES), F32) for _ in range(3 * len(DSW_PATTERNS))],
        compiler_params=_params("parallel", "parallel"),
        name="dsw",
    )(bq, bk, bv)


def _ffn_kernel(h_ref, ma_ref, mb_ref, wo_ref, gf_ref, wg_ref, wv_ref, cg_ref, cv_ref, wd_ref, gfin_ref,
                o_ref, carry_ref, ubuf_ref, xn_ref, acc_ref, *, tiles_per_seq, final_norm):
    tm = h_ref.shape[0]
    half = ma_ref.shape[1]
    nchunks = wg_ref.shape[0]
    H = SUBLANES

    @pl.when(pl.program_id(0) % tiles_per_seq == 0)
    def _():
        carry_ref[...] = jnp.zeros_like(carry_ref)

    h1 = h_ref[...] + _dot(ma_ref[...], wo_ref[0:half, :]) + _dot(mb_ref[...], wo_ref[half:, :])
    xn_ref[...] = _rmsnorm(h1, gf_ref[...]).astype(BF16)
    acc_ref[...] = h1

    def conv(u, w, slot, c):
        buf = ubuf_ref.at[slot]
        buf[0:H, :] = carry_ref[c, slot]
        buf[H:, :] = u
        carry_ref[c, slot] = buf[tm:tm + H, :]
        out = w[FFN_CONV - 1:FFN_CONV, :] * u
        for j in range(1, FFN_CONV):
            out = out + w[FFN_CONV - 1 - j:FFN_CONV - j, :] * buf[pl.ds(H - j, tm), :]
        return out

    def body(c, _):
        xn = xn_ref[...]
        g = conv(_dot(xn, wg_ref[c]), cg_ref[c], 0, c)
        v = conv(_dot(xn, wv_ref[c]), cv_ref[c], 1, c)
        act = (_silu(g) * v).astype(BF16)
        acc_ref[...] += _dot(act, wd_ref[c])
        return 0

    lax.fori_loop(0, nchunks, body, 0)
    out = acc_ref[...]
    if final_norm:
        out = _rmsnorm(out, gfin_ref[...])
    o_ref[...] = out


def _outproj_ffn(h2, mix_a, mix_b, w_out, g_ffn, w_up, w_conv, w_down, g_final, seq, final_norm):
    T = h2.shape[0]
    tm, tf = FFN_TM, FFN_TF
    nchunks = D_FF // tf
    half = mix_a.shape[1]
    chunked_cols = lambda w: w.reshape(w.shape[0], nchunks, tf).transpose(1, 0, 2)
    wg = chunked_cols(w_up[:, :D_FF]).astype(BF16)
    wv = chunked_cols(w_up[:, D_FF:]).astype(BF16)
    pad = jnp.zeros((SUBLANES - FFN_CONV, 2 * D_FF), F32)
    wc = jnp.concatenate([w_conv, pad], axis=0)
    cg = chunked_cols(wc[:, :D_FF])
    cv = chunked_cols(wc[:, D_FF:])
    wd = w_down.reshape(nchunks, tf, D_MODEL).astype(BF16)
    row = lambda n: pl.BlockSpec((tm, n), lambda i: (i, 0))
    kern = functools.partial(_ffn_kernel, tiles_per_seq=seq // tm, final_norm=final_norm)
    return pl.pallas_call(
        kern,
        grid=(T // tm,),
        in_specs=[row(D_MODEL), row(half), row(half), _resident((2 * half, D_MODEL)), _resident((1, D_MODEL)),
                  _resident(wg.shape), _resident(wv.shape), _resident(cg.shape), _resident(cv.shape),
                  _resident(wd.shape), _resident((1, D_MODEL))],
        out_specs=row(D_MODEL),
        out_shape=jax.ShapeDtypeStruct((T, D_MODEL), F32),
        scratch_shapes=[pltpu.VMEM((nchunks, 2, SUBLANES, tf), F32),
                        pltpu.VMEM((2, tm + SUBLANES, tf), F32),
                        pltpu.VMEM((tm, D_MODEL), BF16),
                        pltpu.VMEM((tm, D_MODEL), F32)],
        compiler_params=_params("arbitrary"),
        name="outproj_ffn",
    )(h2, mix_a, mix_b, w_out.astype(BF16), g_ffn.reshape(1, -1), wg, wv, cg, cv, wd, g_final.reshape(1, -1))


def _inproj1_kernel(x_ref, g_ref, w_ref, c_ref, q_ref, k_ref, v_ref):
    xn = _rmsnorm(x_ref[...], g_ref[...]).astype(BF16)

    def proj(lo, hi):
        return _dot(xn, w_ref[:, lo:hi])

    c_ref[...] = proj(0, 512) * jax.nn.sigmoid(proj(512, 1024))
    q_ref[...] = (proj(1024, 1536) * (HEAD_DIM ** -0.5)).astype(BF16)
    k_ref[...] = proj(1536, 2048).astype(BF16)
    v_ref[...] = proj(2048, 2560).astype(BF16)


def _inproj1(x2, g, w_in):
    T = x2.shape[0]
    tm = PROJ_TM
    row = lambda n: pl.BlockSpec((tm, n), lambda i: (i, 0))
    outs = [(CONV_CH, F32), (512, BF16), (512, BF16), (512, BF16)]
    return pl.pallas_call(
        _inproj1_kernel,
        grid=(T // tm,),
        in_specs=[row(D_MODEL), _resident((1, D_MODEL)), _resident(w_in.shape)],
        out_specs=[row(n) for n, _ in outs],
        out_shape=[jax.ShapeDtypeStruct((T, n), dt) for n, dt in outs],
        compiler_params=_params("parallel"),
        name="inproj1",
    )(x2, g.reshape(1, -1), w_in.astype(BF16))


def _convmod_kernel(c_ref, w_ref, b_ref, lg_ref, lb_ref, o_ref, buf_ref):
    ts = c_ref.shape[0]
    HALO = CONV_HALO

    @pl.when(pl.program_id(1) == 0)
    def _():
        buf_ref[0:HALO, :] = jnp.zeros((HALO, CONV_CH), F32)

    @pl.when(pl.program_id(1) > 0)
    def _():
        buf_ref[0:HALO, :] = buf_ref[ts:ts + HALO, :]

    buf_ref[HALO:, :] = c_ref[...]
    bias, lg, lb = b_ref[...], lg_ref[...], lb_ref[...]
    R = CONV_RB
    for rb in range(ts // R):
        acc = jnp.zeros((R, CONV_CH), F32)
        for kk in range(CONV_WIDTH):
            off = rb * R + HALO - (CONV_WIDTH - 1) + kk
            acc = acc + w_ref[kk:kk + 1, :] * buf_ref[off:off + R, :]
        y = acc + bias
        mu = jnp.mean(y, axis=-1, keepdims=True)
        yc = y - mu
        var = jnp.mean(yc * yc, axis=-1, keepdims=True)
        z = yc * lax.rsqrt(var + EPS) * lg + lb
        o_ref[rb * R:(rb + 1) * R, :] = _silu(z).astype(o_ref.dtype)


def _convmod(c, conv_w, conv_b, ln_g, ln_b, batch, seq):
    ts = CONV_TS
    spb = seq // ts
    wpad = jnp.concatenate([conv_w, jnp.zeros((32 - CONV_WIDTH, CONV_CH), F32)], axis=0)
    spec = pl.BlockSpec((ts, CONV_CH), lambda b, s: (b * spb + s, 0))
    vec = _resident((1, CONV_CH))
    return pl.pallas_call(
        _convmod_kernel,
        grid=(batch, spb),
        in_specs=[spec, _resident(wpad.shape), vec, vec, vec],
        out_specs=spec,
        out_shape=jax.ShapeDtypeStruct((batch * seq, CONV_CH), BF16),
        scratch_shapes=[pltpu.VMEM((CONV_HALO + ts, CONV_CH), F32)],
        compiler_params=_params("parallel", "arbitrary"),
        name="convmod",
    )(c, wpad, conv_b.reshape(1, -1), ln_g.reshape(1, -1), ln_b.reshape(1, -1))


def _sb_kernel(q_ref, k_ref, v_ref, u2_ref, o_ref, carry_ref, acc_ref):
    TQ, TK, NSUB = SB_TQ, SB_TK, SB_NSUB
    qsuper = pl.program_id(2)
    head0 = _head0_lanes((TQ, LANES))
    qpos = lax.broadcasted_iota(jnp.int32, (TQ, TK), 0)
    kpos = lax.broadcasted_iota(jnp.int32, (TQ, TK), 1)
    below_diag = kpos < qpos
    carry_ref[...] = jnp.zeros_like(carry_ref)
    acc_ref[...] = jnp.zeros_like(acc_ref)

    def key_step(kb, subs):
        rows = pl.ds(pl.multiple_of(kb * TK, TK), TK)
        k_blk, v_blk = k_ref[rows, :], v_ref[rows, :]
        chains = [(sub, h, masked) for sub, masked in subs for h in range(2)]
        zs = []
        for sub, h, _ in chains:
            q = q_ref[sub * TQ:(sub + 1) * TQ, :]
            lanes_h = head0 if h == 0 else jnp.logical_not(head0)
            zs.append(_dot_nt(jnp.where(lanes_h, q, jnp.zeros_like(q)), k_blk))
        splits = []
        for (sub, h, masked), z in zip(chains, zs):
            neg_abs = lax.bitcast_convert_type(lax.bitcast_convert_type(z, jnp.uint32) | jnp.uint32(SIGN_BIT), F32)
            nl = jnp.maximum(z, 0.0) + jnp.log(1.0 + jnp.exp(neg_abs))
            if masked:
                nl = jnp.where(below_diag, nl, 0.0)
            hi = nl.astype(BF16)
            lo = (nl - hi.astype(F32)).astype(BF16)
            splits.append(jnp.concatenate([hi, lo], axis=1))
        incls = [_dot(hl, u2_ref[...]) for hl in splits]
        probs = []
        for (sub, h, masked), z, incl in zip(chains, zs, incls):
            carry = carry_ref[sub, h]
            a = jnp.exp(z - (incl + jnp.concatenate([carry] * (TK // LANES), axis=1)))
            if masked:
                a = jnp.where(below_diag, a, 0.0)
            carry_ref[sub, h] = carry + jnp.broadcast_to(incl[:, 0:1], (TQ, LANES))
            probs.append(a.astype(BF16))
        for (sub, h, _), a in zip(chains, probs):
            acc_ref[sub, h] += _dot(a, v_blk)

    for j in reversed(range(NSUB)):
        key_step(qsuper * NSUB + j, [(sub, sub == j) for sub in range(j, NSUB)])

    def body(i, _):
        key_step(qsuper * NSUB - 1 - i, [(sub, False) for sub in range(NSUB)])
        return 0

    lax.fori_loop(0, qsuper * NSUB, body, 0)
    for sub in range(NSUB):
        out = jnp.where(head0, acc_ref[sub, 0], acc_ref[sub, 1])
        o_ref[sub * TQ:(sub + 1) * TQ, :] = out.astype(o_ref.dtype)


def _sb(dq, dk, dv, batch, seq):
    TQ, TK, NSUB = SB_TQ, SB_TK, SB_NSUB
    assert TQ == TK
    npairs = SB_HEADS // 2
    nq = seq // (TQ * NSUB)
    j = lax.broadcasted_iota(jnp.int32, (TK, TK), 0)
    s = lax.broadcasted_iota(jnp.int32, (TK, TK), 1)
    u = jnp.where(j >= s, 1.0, 0.0).astype(BF16)
    u2 = jnp.concatenate([u, u], axis=0)
    qspec = pl.BlockSpec((TQ * NSUB, LANES), lambda b, p, i: (b * nq + i, p))
    kvspec = pl.BlockSpec((seq, LANES), lambda b, p, i: (b, p))
    return pl.pallas_call(
        _sb_kernel,
        grid=(batch, npairs, nq),
        in_specs=[qspec, kvspec, kvspec, _resident((2 * TK, TK))],
        out_specs=qspec,
        out_shape=jax.ShapeDtypeStruct((batch * seq, SB_HEADS * HEAD_DIM), BF16),
        scratch_shapes=[pltpu.VMEM((NSUB, 2, TQ, LANES), F32), pltpu.VMEM((NSUB, 2, TQ, LANES), F32)],
        compiler_params=_params("parallel", "parallel", "arbitrary"),
        name="stickbreak",
    )(dq, dk, dv, u2)


def kernel(x, norm_mix0, w_in0, gla_wa2, gla_ba, gla_norm, w_out0, norm_ffn0, ffn_up0, ffn_conv0, ffn_down0,
           norm_mix1, w_in1, conv_w1, conv_b1, conv_ln_g1, conv_ln_b1, w_out1, norm_ffn1, ffn_up1, ffn_conv1,
           ffn_down1, final_norm):
    batch, seq, d = x.shape
    h = x.reshape(batch * seq, d)
    aq, ak, av, ag, la, bq, bk, bv = _inproj0(h, norm_mix0, w_in0, gla_wa2, gla_ba, seq)
    oa = _gla(aq, ak, la, av, ag, gla_norm, batch, seq)
    ob = _dsw(bq, bk, bv, batch, seq)
    h = _outproj_ffn(h, oa, ob, w_out0, norm_ffn0, ffn_up0, ffn_conv0, ffn_down0, final_norm, seq, False)
    c, dq, dk, dv = _inproj1(h, norm_mix1, w_in1)
    oc = _convmod(c, conv_w1, conv_b1, conv_ln_g1, conv_ln_b1, batch, seq)
    od = _sb(dq, dk, dv, batch, seq)
    h = _outproj_ffn(h, oc, od, w_out1, norm_ffn1, ffn_up1, ffn_conv1, ffn_down1, final_norm, seq, True)
    return h.reshape(batch, seq, d)
```

```python
import functools

import jax
import jax.numpy as jnp
from jax import lax
from jax.experimental import pallas as pl
from jax.experimental.pallas import tpu as pltpu

F32 = jnp.float32
BF16 = jnp.bfloat16

D_MODEL = 1024
HEAD_DIM = 64
EPS = 1e-6
GLA_HEADS = 4
GLA_DV = 128
GLA_DK = 64
GLA_RANK = 16
GLA_TAU = 16.0
GLA_CHUNK = 64
GLA_GROUP = 4
DSW_HEADS = 8
DSW_PATTERNS = ((128, 1), (512, 4), (2048, 16))
DSW_BLOCK = 128
CONV_CH = 512
CONV_WIDTH = 31
SB_HEADS = 8
D_FF = 2816
FFN_CONV = 3
ROPE_THETA = 500000.0
ROPE_DIMS = 16

LANES = 128
SUBLANES = 8
VMEM_LIMIT = 56 * 1024 * 1024
MASK_NEG = -1e30
SIGN_BIT = 0x80000000

PROJ_TM = 512
FFN_TM = 512
FFN_TF = 256
CONV_TS = 512
CONV_HALO = 32
CONV_RB = 32
SB_TQ = 256
SB_TK = 256
SB_NSUB = 4
DSW_GROUP = 4
DSW_COMBINE_ROWS = 256


def _dot(a, b):
    return jnp.dot(a, b, preferred_element_type=F32)


def _dot_nt(a, b):
    return lax.dot_general(a, b, (((1,), (1,)), ((), ())), preferred_element_type=F32)


def _dot_tn(a, b):
    return lax.dot_general(a, b, (((0,), (0,)), ((), ())), preferred_element_type=F32)


def _rmsnorm(x, g):
    return x * lax.rsqrt(jnp.mean(x * x, axis=-1, keepdims=True) + EPS) * g


def _softplus_neg_abs(z):
    return jnp.log1p(jnp.exp(-jnp.abs(z)))


def _log_sigmoid(z):
    return jnp.minimum(z, 0.0) - _softplus_neg_abs(z)


def _silu(x):
    return x * jax.nn.sigmoid(x)


def _head0_lanes(shape):
    return lax.broadcasted_iota(jnp.int32, shape, len(shape) - 1) < HEAD_DIM


def _resident(shape):
    nd = len(shape)
    return pl.BlockSpec(shape, lambda *_: (0,) * nd, pipeline_mode=pl.Buffered(1))


def _params(*sem):
    return pltpu.CompilerParams(dimension_semantics=sem, vmem_limit_bytes=VMEM_LIMIT)


def _rope(x, cos, sin_lo, sin_hi):
    half = ROPE_DIMS // 2
    return x * cos + pltpu.roll(x, LANES - half, 1) * sin_lo + pltpu.roll(x, half, 1) * sin_hi


def _inproj0_kernel(x_ref, g_ref, w_ref, wa2_ref, ba_ref, cos_ref, sinlo_ref, sinhi_ref,
                    aq_ref, ak_ref, av_ref, ag_ref, la_ref, bq_ref, bk_ref, bv_ref):
    xn = _rmsnorm(x_ref[...], g_ref[...]).astype(BF16)

    def proj(lo, hi):
        return _dot(xn, w_ref[:, lo:hi])

    aq_ref[...] = (proj(0, 256) * (GLA_DK ** -0.5)).astype(BF16)
    ak_ref[...] = proj(256, 512).astype(BF16)
    av_ref[...] = proj(512, 1024).astype(BF16)
    ag_ref[...] = proj(1024, 1536)
    ar = proj(3072, 3200)
    gate_logit = _dot(ar.astype(BF16), wa2_ref[...]) + ba_ref[...]
    la_ref[...] = _log_sigmoid(gate_logit) / GLA_TAU
    cos, sin_lo, sin_hi = cos_ref[...], sinlo_ref[...], sinhi_ref[...]
    for blk in range(4):
        lo = 1536 + blk * LANES
        sl = slice(blk * LANES, (blk + 1) * LANES)
        bq_ref[:, sl] = _rope(proj(lo, lo + LANES), cos, sin_lo, sin_hi) * (HEAD_DIM ** -0.5)
        bk_ref[:, sl] = _rope(proj(lo + 512, lo + 512 + LANES), cos, sin_lo, sin_hi)
    bv_ref[...] = proj(2560, 3072)


def _rope_tables(seq):
    half = ROPE_DIMS // 2
    inv = ROPE_THETA ** (-jnp.arange(half, dtype=F32) / half)
    ang = jnp.arange(seq, dtype=F32)[:, None] * inv[None, :]
    cos, sin = jnp.cos(ang), jnp.sin(ang)
    ones = jnp.ones((seq, HEAD_DIM - ROPE_DIMS), F32)
    zeros = jnp.zeros((seq, HEAD_DIM - ROPE_DIMS), F32)
    zh = jnp.zeros((seq, half), F32)
    cos_h = jnp.concatenate([cos, cos, ones], axis=1)
    lo_h = jnp.concatenate([-sin, zh, zeros], axis=1)
    hi_h = jnp.concatenate([zh, sin, zeros], axis=1)
    two = lambda t: jnp.concatenate([t, t], axis=1)
    return two(cos_h), two(lo_h), two(hi_h)


def _inproj0(x2, g, w_in, wa2, ba, seq):
    T = x2.shape[0]
    tm = PROJ_TM
    aw_k, aw_v, bw = GLA_HEADS * GLA_DK, GLA_HEADS * GLA_DV, DSW_HEADS * HEAD_DIM
    o = [0, aw_k, 2 * aw_k, 2 * aw_k + aw_v, 2 * aw_k + 2 * aw_v]
    o.append(o[-1] + GLA_RANK)
    ar_cols = w_in[:, o[4]:o[5]]
    w = jnp.concatenate(
        [w_in[:, :o[4]], w_in[:, o[5]:], ar_cols, jnp.zeros((D_MODEL, LANES - GLA_RANK), F32)], axis=1
    ).astype(BF16)
    wa2p = jnp.concatenate([wa2, jnp.zeros((LANES - GLA_RANK, aw_k), F32)], axis=0).astype(BF16)
    cos, sin_lo, sin_hi = _rope_tables(seq)
    spt = seq // tm
    row = lambda n: pl.BlockSpec((tm, n), lambda i: (i, 0))
    tab = pl.BlockSpec((tm, LANES), lambda i: (i % spt, 0))
    outs = [(aw_k, BF16), (aw_k, BF16), (aw_v, BF16), (aw_v, F32), (aw_k, F32), (bw, F32), (bw, F32), (bw, F32)]
    return pl.pallas_call(
        _inproj0_kernel,
        grid=(T // tm,),
        in_specs=[row(D_MODEL), _resident((1, D_MODEL)), _resident(w.shape), _resident(wa2p.shape),
                  _resident((1, aw_k)), tab, tab, tab],
        out_specs=[row(n) for n, _ in outs],
        out_shape=[jax.ShapeDtypeStruct((T, n), dt) for n, dt in outs],
        compiler_params=_params("parallel"),
        name="inproj0",
    )(x2, g.reshape(1, -1), w, wa2p, ba.reshape(1, -1), cos, sin_lo, sin_hi)


def _gla_kernel(q_ref, k_ref, la_ref, v_ref, gate_ref, gn_ref, o_ref, st0_ref, st1_ref):
    C, G = GLA_CHUNK, GLA_GROUP
    seq = q_ref.shape[0]
    st_refs = (st0_ref, st1_ref)
    for st in st_refs:
        st[...] = jnp.zeros_like(st)
    head0 = _head0_lanes((C, LANES))
    r = lax.broadcasted_iota(jnp.int32, (C, C), 0)
    c = lax.broadcasted_iota(jnp.int32, (C, C), 1)
    tril = c <= r
    tri = jnp.where(tril, 1.0, 0.0).astype(BF16)
    tri3 = jnp.concatenate([tri, tri, tri], axis=1)
    gn = gn_ref[...]

    def body(it, _):
        rows = [pl.ds(pl.multiple_of((it * G + j) * C, C), C) for j in range(G)]
        la = jnp.concatenate([la_ref[rw, :] for rw in rows], axis=1)
        hi = la.astype(BF16)
        rem = la - hi.astype(F32)
        mid = rem.astype(BF16)
        lo = (rem - mid.astype(F32)).astype(BF16)
        bcum_all = _dot(tri3, jnp.concatenate([hi, mid, lo], axis=0))
        chains, decs = [], []
        for j in range(G):
            bcum = bcum_all[:, j * LANES:(j + 1) * LANES]
            btot = bcum[C - 1:C, :]
            q = q_ref[rows[j], :].astype(F32)
            k = k_ref[rows[j], :].astype(F32)
            q_dec = q * jnp.exp(bcum)
            k_inv = (k * jnp.exp(-bcum)).astype(BF16)
            k_tail = k * jnp.exp(btot - bcum)
            decs.append(jnp.exp(btot))
            for h in range(2):
                lanes_h = head0 if h == 0 else jnp.logical_not(head0)
                qh = jnp.where(lanes_h, q_dec, 0.0).astype(BF16)
                kth = jnp.where(lanes_h, k_tail, 0.0).astype(BF16)
                vh = v_ref[rows[j], h * GLA_DV:(h + 1) * GLA_DV]
                chains.append((j, h, qh, k_inv, kth, vh))
        scores = [jnp.where(tril, _dot_nt(qh, k_inv), 0.0).astype(BF16) for _, _, qh, k_inv, _, _ in chains]
        incs = [_dot_tn(vh, kth) for _, _, _, _, kth, vh in chains]
        states = {}
        for h in range(2):
            st = st_refs[h][...]
            for j in range(G):
                states[j, h] = st.astype(BF16)
                st = st * decs[j] + incs[2 * j + h]
            st_refs[h][...] = st
        outs = [_dot(sc, vh) + _dot_nt(qh, states[j, h])
                for sc, (j, h, qh, _, _, vh) in zip(scores, chains)]
        for o, (j, h, _, _, _, _) in zip(outs, chains):
            cols = slice(h * GLA_DV, (h + 1) * GLA_DV)
            y = _rmsnorm(o, gn) * _silu(gate_ref[rows[j], cols])
            o_ref[rows[j], cols] = y.astype(o_ref.dtype)
        return 0

    lax.fori_loop(0, seq // (C * G), body, 0)


def _gla(aq, ak, la, av, ag, gla_norm, batch, seq):
    npairs = GLA_HEADS // 2
    qk = pl.BlockSpec((seq, LANES), lambda b, p: (b, p))
    vv = pl.BlockSpec((seq, 2 * GLA_DV), lambda b, p: (b, p))
    return pl.pallas_call(
        _gla_kernel,
        grid=(batch, npairs),
        in_specs=[qk, qk, qk, vv, vv, _resident((1, GLA_DV))],
        out_specs=vv,
        out_shape=jax.ShapeDtypeStruct((batch * seq, GLA_HEADS * GLA_DV), BF16),
        scratch_shapes=[pltpu.VMEM((GLA_DV, LANES), F32), pltpu.VMEM((GLA_DV, LANES), F32)],
        compiler_params=_params("parallel", "parallel"),
        name="gla",
    )(aq, ak, la, av, ag, gla_norm.reshape(1, -1))


def _dsw_kernel(q_ref, k_ref, v_ref, o_ref, *stat_refs):
    C = DSW_BLOCK
    seq = q_ref.shape[0]
    npat = len(DSW_PATTERNS)
    m_refs, l_refs, a_refs = stat_refs[0:npat], stat_refs[npat:2 * npat], stat_refs[2 * npat:3 * npat]
    head0 = _head0_lanes((C, LANES))
    qi = lax.broadcasted_iota(jnp.int32, (C, 2 * C), 0)
    ki = lax.broadcasted_iota(jnp.int32, (C, 2 * C), 1)
    is_prev = ki < C
    valid_cur = jnp.logical_and(ki >= C, ki - C <= qi)

    def rows(start, d):
        return pl.ds(start, C) if d == 1 else pl.ds(start, C, stride=d)

    for pat, (window, d) in enumerate(DSW_PATTERNS):
        assert window // d == C
        nb = seq // d // C

        def body(it, _, d=d, nb=nb, pat=pat):
            G = DSW_GROUP
            base = it * G
            if nb > 1:
                assert nb % G == 0
                stream, blk0 = base // nb, base % nb
                starts = [stream + d * C * (blk0 + j) for j in range(G)]
                first = rows(jnp.maximum(starts[0] - d * C, 0), d)
                kb = [k_ref[first, :].astype(BF16)] + [k_ref[rows(s, d), :].astype(BF16) for s in starts]
                vb = [v_ref[first, :].astype(BF16)] + [v_ref[rows(s, d), :].astype(BF16) for s in starts]
                kcats = [jnp.concatenate([kb[j], kb[j + 1]], axis=0) for j in range(G)]
                vcats = [jnp.concatenate([vb[j], vb[j + 1]], axis=0) for j in range(G)]
                first_key = qi + jnp.where(blk0 > 0, 0, 2 * C)
                valid0 = jnp.logical_or(jnp.logical_and(is_prev, ki >= first_key), valid_cur)
                valids = [valid0] + [jnp.logical_or(jnp.logical_and(is_prev, ki >= qi), valid_cur)] * (G - 1)
            else:
                starts = [base + j for j in range(G)]
                kcats = [k_ref[rows(s, d), :].astype(BF16) for s in starts]
                vcats = [v_ref[rows(s, d), :].astype(BF16) for s in starts]
                valids = [valid_cur[:, C:]] * G
            scores = []
            for j in range(G):
                q = q_ref[rows(starts[j], d), :]
                for h in range(2):
                    lanes_h = head0 if h == 0 else jnp.logical_not(head0)
                    scores.append(_dot_nt(jnp.where(lanes_h, q, 0.0).astype(BF16), kcats[j]))
            stats = []
            for i, s in enumerate(scores):
                s = jnp.where(valids[i // 2], s, MASK_NEG)
                m = jnp.max(s, axis=-1, keepdims=True)
                p = jnp.exp(s - m)
                stats.append((m, jnp.sum(p, axis=-1, keepdims=True), p.astype(BF16)))
            nums = [_dot(p, vcats[i // 2]) for i, (_, _, p) in enumerate(stats)]
            for j in range(G):
                cur = rows(starts[j], d)
                (m0, den0, _), (m1, den1, _) = stats[2 * j], stats[2 * j + 1]
                m_refs[pat][cur, :] = jnp.where(head0, m0, m1)
                l_refs[pat][cur, :] = jnp.where(head0, den0, den1)
                a_refs[pat][cur, :] = jnp.where(head0, nums[2 * j], nums[2 * j + 1])
            return 0

        lax.fori_loop(0, seq // C // DSW_GROUP, body, 0)

    R = DSW_COMBINE_ROWS
    for r0 in range(0, seq, R):
        rows_r = slice(r0, r0 + R)
        ms = [m[rows_r, :] for m in m_refs]
        m_max = functools.reduce(jnp.maximum, ms)
        ws = [jnp.exp(m - m_max) for m in ms]
        num = sum(w * a[rows_r, :] for w, a in zip(ws, a_refs))
        den = sum(w * l[rows_r, :] for w, l in zip(ws, l_refs))
        o_ref[rows_r, :] = (num / den).astype(o_ref.dtype)


def _dsw(bq, bk, bv, batch, seq):
    npairs = DSW_HEADS // 2
    spec = pl.BlockSpec((seq, LANES), lambda b, p: (b, p))
    return pl.pallas_call(
        _dsw_kernel,
        grid=(batch, npairs),
        in_specs=[spec, spec, spec],
        out_specs=spec,
        out_shape=jax.ShapeDtypeStruct((batch * seq, DSW_HEADS * HEAD_DIM), BF16),
        scratch_shapes=[pltpu.VMEM((seq, LANES), F32) for _ in range(3 * len(DSW_PATTERNS))],
        compiler_params=_params("parallel", "parallel"),
        name="dsw",
    )(bq, bk, bv)


def _ffn_kernel(h_ref, ma_ref, mb_ref, wo_ref, gf_ref, wg_ref, wv_ref, cg_ref, cv_ref, wd_ref, gfin_ref,
                o_ref, carry_ref, ubuf_ref, xn_ref, act_ref, *, tiles_per_seq, final_norm):
    tm = h_ref.shape[0]
    half = ma_ref.shape[1]
    nchunks, _, tf = wg_ref.shape
    H = SUBLANES

    @pl.when(pl.program_id(0) % tiles_per_seq == 0)
    def _():
        carry_ref[...] = jnp.zeros_like(carry_ref)

    h1 = h_ref[...] + _dot(ma_ref[...], wo_ref[0:half, :]) + _dot(mb_ref[...], wo_ref[half:, :])
    xn_ref[...] = _rmsnorm(h1, gf_ref[...]).astype(BF16)
    o_ref[...] = h1

    def up(c):
        for slot, w_ref in enumerate((wg_ref, wv_ref)):
            buf = ubuf_ref.at[c % 2, slot]
            buf[0:H, :] = carry_ref[c, slot]
            buf[H:, :] = _dot(xn_ref[...], w_ref[c])
            carry_ref[c, slot] = buf[tm:tm + H, :]

    def conv(c, slot, w):
        buf = ubuf_ref.at[c % 2, slot]
        out = w[FFN_CONV - 1:FFN_CONV, :] * buf[H:, :]
        for j in range(1, FFN_CONV):
            out = out + w[FFN_CONV - 1 - j:FFN_CONV - j, :] * buf[H - j:H - j + tm, :]
        return out

    up(0)
    for c in range(nchunks):
        if c + 1 < nchunks:
            up(c + 1)
        act = _silu(conv(c, 0, cg_ref[c])) * conv(c, 1, cv_ref[c])
        act_ref[:, c * tf:(c + 1) * tf] = act.astype(BF16)
    out = o_ref[...] + _dot(act_ref[...], wd_ref[...])
    if final_norm:
        out = _rmsnorm(out, gfin_ref[...])
    o_ref[...] = out


def _outproj_ffn(h2, mix_a, mix_b, w_out, g_ffn, w_up, w_conv, w_down, g_final, seq, final_norm):
    T = h2.shape[0]
    tm, tf = FFN_TM, FFN_TF
    nchunks = D_FF // tf
    half = mix_a.shape[1]
    chunked_cols = lambda w: w.reshape(w.shape[0], nchunks, tf).transpose(1, 0, 2)
    wg = chunked_cols(w_up[:, :D_FF]).astype(BF16)
    wv = chunked_cols(w_up[:, D_FF:]).astype(BF16)
    pad = jnp.zeros((SUBLANES - FFN_CONV, 2 * D_FF), F32)
    wc = jnp.concatenate([w_conv, pad], axis=0)
    cg = chunked_cols(wc[:, :D_FF])
    cv = chunked_cols(wc[:, D_FF:])
    wd = w_down.astype(BF16)
    row = lambda n: pl.BlockSpec((tm, n), lambda i: (i, 0))
    kern = functools.partial(_ffn_kernel, tiles_per_seq=seq // tm, final_norm=final_norm)
    return pl.pallas_call(
        kern,
        grid=(T // tm,),
        in_specs=[row(D_MODEL), row(half), row(half), _resident((2 * half, D_MODEL)), _resident((1, D_MODEL)),
                  _resident(wg.shape), _resident(wv.shape), _resident(cg.shape), _resident(cv.shape),
                  _resident(wd.shape), _resident((1, D_MODEL))],
        out_specs=row(D_MODEL),
        out_shape=jax.ShapeDtypeStruct((T, D_MODEL), F32),
        scratch_shapes=[pltpu.VMEM((nchunks, 2, SUBLANES, tf), F32),
                        pltpu.VMEM((2, 2, tm + SUBLANES, tf), F32),
                        pltpu.VMEM((tm, D_MODEL), BF16),
                        pltpu.VMEM((tm, D_FF), BF16)],
        compiler_params=_params("arbitrary"),
        name="outproj_ffn",
    )(h2, mix_a, mix_b, w_out.astype(BF16), g_ffn.reshape(1, -1), wg, wv, cg, cv, wd, g_final.reshape(1, -1))


def _inproj1_kernel(x_ref, g_ref, w_ref, c_ref, q_ref, k_ref, v_ref):
    xn = _rmsnorm(x_ref[...], g_ref[...]).astype(BF16)

    def proj(lo, hi):
        return _dot(xn, w_ref[:, lo:hi])

    c_ref[...] = proj(0, 512) * jax.nn.sigmoid(proj(512, 1024))
    q_ref[...] = (proj(1024, 1536) * (HEAD_DIM ** -0.5)).astype(BF16)
    k_ref[...] = proj(1536, 2048).astype(BF16)
    v_ref[...] = proj(2048, 2560).astype(BF16)


def _inproj1(x2, g, w_in):
    T = x2.shape[0]
    tm = PROJ_TM
    row = lambda n: pl.BlockSpec((tm, n), lambda i: (i, 0))
    outs = [(CONV_CH, F32), (512, BF16), (512, BF16), (512, BF16)]
    return pl.pallas_call(
        _inproj1_kernel,
        grid=(T // tm,),
        in_specs=[row(D_MODEL), _resident((1, D_MODEL)), _resident(w_in.shape)],
        out_specs=[row(n) for n, _ in outs],
        out_shape=[jax.ShapeDtypeStruct((T, n), dt) for n, dt in outs],
        compiler_params=_params("parallel"),
        name="inproj1",
    )(x2, g.reshape(1, -1), w_in.astype(BF16))


def _convmod_kernel(c_ref, w_ref, b_ref, lg_ref, lb_ref, o_ref, buf_ref, sh_ref):
    ts = c_ref.shape[0]
    HALO = CONV_HALO

    @pl.when(pl.program_id(1) == 0)
    def _():
        buf_ref[0:HALO, :] = jnp.zeros((HALO, CONV_CH), F32)

    @pl.when(pl.program_id(1) > 0)
    def _():
        buf_ref[0:HALO, :] = buf_ref[ts:ts + HALO, :]

    buf_ref[HALO:, :] = c_ref[...]
    n_sh = ts + HALO - SUBLANES
    for b in range(1, SUBLANES):
        sh_ref[b, 0:n_sh, :] = buf_ref[b:b + n_sh, :]
    bias, lg, lb = b_ref[...], lg_ref[...], lb_ref[...]
    R = CONV_RB

    def body(rb, _):
        r0 = pl.multiple_of(rb * R, R)
        accs = [jnp.zeros((R // SUBLANES, SUBLANES, CONV_CH), F32) for _ in range(2)]
        for kk in range(CONV_WIDTH):
            off = HALO - (CONV_WIDTH - 1) + kk
            a, b = off // SUBLANES, off % SUBLANES
            rows = pl.ds(r0 + a * SUBLANES, R)
            x = buf_ref[rows, :] if b == 0 else sh_ref[b, rows, :]
            accs[kk % 2] = accs[kk % 2] + w_ref[kk][None] * x.reshape(R // SUBLANES, SUBLANES, CONV_CH)
        acc = (accs[0] + accs[1]).reshape(R, CONV_CH)
        y = acc + bias
        mu = jnp.mean(y, axis=-1, keepdims=True)
        yc = y - mu
        var = jnp.mean(yc * yc, axis=-1, keepdims=True)
        z = yc * lax.rsqrt(var + EPS) * lg + lb
        o_ref[pl.ds(r0, R), :] = _silu(z).astype(o_ref.dtype)
        return 0

    lax.fori_loop(0, ts // R, body, 0, unroll=2)


def _convmod(c, conv_w, conv_b, ln_g, ln_b, batch, seq):
    ts = CONV_TS
    spb = seq // ts
    wrep = jnp.broadcast_to(conv_w[:, None, :], (CONV_WIDTH, SUBLANES, CONV_CH))
    spec = pl.BlockSpec((ts, CONV_CH), lambda b, s: (b * spb + s, 0))
    vec = _resident((1, CONV_CH))
    return pl.pallas_call(
        _convmod_kernel,
        grid=(batch, spb),
        in_specs=[spec, _resident(wrep.shape), vec, vec, vec],
        out_specs=spec,
        out_shape=jax.ShapeDtypeStruct((batch * seq, CONV_CH), BF16),
        scratch_shapes=[pltpu.VMEM((CONV_HALO + ts, CONV_CH), F32),
                        pltpu.VMEM((SUBLANES, CONV_HALO + ts, CONV_CH), F32)],
        compiler_params=_params("parallel", "arbitrary"),
        name="convmod",
    )(c, wrep, conv_b.reshape(1, -1), ln_g.reshape(1, -1), ln_b.reshape(1, -1))


def _sb_kernel(q_ref, k_ref, v_ref, u2_ref, o_ref, carry_ref, acc_ref):
    TQ, TK, NSUB = SB_TQ, SB_TK, SB_NSUB
    qsuper = pl.program_id(2)
    head0 = _head0_lanes((TQ, LANES))
    qpos = lax.broadcasted_iota(jnp.int32, (TQ, TK), 0)
    kpos = lax.broadcasted_iota(jnp.int32, (TQ, TK), 1)
    below_diag = kpos < qpos
    carry_ref[...] = jnp.zeros_like(carry_ref)
    acc_ref[...] = jnp.zeros_like(acc_ref)

    def key_step(kb, subs):
        rows = pl.ds(pl.multiple_of(kb * TK, TK), TK)
        k_blk, v_blk = k_ref[rows, :], v_ref[rows, :]
        chains = [(sub, h, masked) for sub, masked in subs for h in range(2)]
        zs = []
        for sub, h, _ in chains:
            q = q_ref[sub * TQ:(sub + 1) * TQ, :]
            lanes_h = head0 if h == 0 else jnp.logical_not(head0)
            zs.append(_dot_nt(jnp.where(lanes_h, q, jnp.zeros_like(q)), k_blk))
        splits = []
        for (sub, h, masked), z in zip(chains, zs):
            neg_abs = lax.bitcast_convert_type(lax.bitcast_convert_type(z, jnp.uint32) | jnp.uint32(SIGN_BIT), F32)
            nl = jnp.maximum(z, 0.0) + jnp.log(1.0 + jnp.exp(neg_abs))
            if masked:
                nl = jnp.where(below_diag, nl, 0.0)
            hi = nl.astype(BF16)
            lo = (nl - hi.astype(F32)).astype(BF16)
            splits.append(jnp.concatenate([hi, lo], axis=1))
        incls = [_dot(hl, u2_ref[...]) for hl in splits]
        probs = []
        for (sub, h, masked), z, incl in zip(chains, zs, incls):
            carry = carry_ref[sub, h]
            a = jnp.exp(z - (incl + jnp.concatenate([carry] * (TK // LANES), axis=1)))
            if masked:
                a = jnp.where(below_diag, a, 0.0)
            carry_ref[sub, h] = carry + jnp.broadcast_to(incl[:, 0:1], (TQ, LANES))
            probs.append(a.astype(BF16))
        for (sub, h, _), a in zip(chains, probs):
            acc_ref[sub, h] += _dot(a, v_blk)

    for j in reversed(range(NSUB)):
        key_step(qsuper * NSUB + j, [(sub, sub == j) for sub in range(j, NSUB)])

    def body(i, _):
        key_step(qsuper * NSUB - 1 - i, [(sub, False) for sub in range(NSUB)])
        return 0

    lax.fori_loop(0, qsuper * NSUB, body, 0)
    for sub in range(NSUB):
        out = jnp.where(head0, acc_ref[sub, 0], acc_ref[sub, 1])
        o_ref[sub * TQ:(sub + 1) * TQ, :] = out.astype(o_ref.dtype)


def _sb(dq, dk, dv, batch, seq):
    TQ, TK, NSUB = SB_TQ, SB_TK, SB_NSUB
    assert TQ == TK
    npairs = SB_HEADS // 2
    nq = seq // (TQ * NSUB)
    j = lax.broadcasted_iota(jnp.int32, (TK, TK), 0)
    s = lax.broadcasted_iota(jnp.int32, (TK, TK), 1)
    u = jnp.where(j >= s, 1.0, 0.0).astype(BF16)
    u2 = jnp.concatenate([u, u], axis=0)
    qspec = pl.BlockSpec((TQ * NSUB, LANES), lambda b, p, i: (b * nq + i, p))
    kvspec = pl.BlockSpec((seq, LANES), lambda b, p, i: (b, p))
    return pl.pallas_call(
        _sb_kernel,
        grid=(batch, npairs, nq),
        in_specs=[qspec, kvspec, kvspec, _resident((2 * TK, TK))],
        out_specs=qspec,
        out_shape=jax.ShapeDtypeStruct((batch * seq, SB_HEADS * HEAD_DIM), BF16),
        scratch_shapes=[pltpu.VMEM((NSUB, 2, TQ, LANES), F32), pltpu.VMEM((NSUB, 2, TQ, LANES), F32)],
        compiler_params=_params("parallel", "parallel", "arbitrary"),
        name="stickbreak",
    )(dq, dk, dv, u2)


def kernel(x, norm_mix0, w_in0, gla_wa2, gla_ba, gla_norm, w_out0, norm_ffn0, ffn_up0, ffn_conv0, ffn_down0,
           norm_mix1, w_in1, conv_w1, conv_b1, conv_ln_g1, conv_ln_b1, w_out1, norm_ffn1, ffn_up1, ffn_conv1,
           ffn_down1, final_norm):
    batch, seq, d = x.shape
    h = x.reshape(batch * seq, d)
    aq, ak, av, ag, la, bq, bk, bv = _inproj0(h, norm_mix0, w_in0, gla_wa2, gla_ba, seq)
    oa = _gla(aq, ak, la, av, ag, gla_norm, batch, seq)
    ob = _dsw(bq, bk, bv, batch, seq)
    h = _outproj_ffn(h, oa, ob, w_out0, norm_ffn0, ffn_up0, ffn_conv0, ffn_down0, final_norm, seq, False)
    c, dq, dk, dv = _inproj1(h, norm_mix1, w_in1)
    oc = _convmod(c, conv_w1, conv_b1, conv_ln_g1, conv_ln_b1, batch, seq)
    od = _sb(dq, dk, dv, batch, seq)
    h = _outproj_ffn(h, oc, od, w_out1, norm_ffn1, ffn_up1, ffn_conv1, ffn_down1, final_norm, seq, True)
    return h.reshape(batch, seq, d)
```

```python
import functools

import jax
import jax.numpy as jnp
from jax import lax
from jax.experimental import pallas as pl
from jax.experimental.pallas import tpu as pltpu

F32 = jnp.float32
BF16 = jnp.bfloat16

D_MODEL = 1024
HEAD_DIM = 64
EPS = 1e-6
GLA_HEADS = 4
GLA_DV = 128
GLA_DK = 64
GLA_RANK = 16
GLA_TAU = 16.0
GLA_CHUNK = 64
GLA_GROUP = 4
DSW_HEADS = 8
DSW_PATTERNS = ((128, 1), (512, 4), (2048, 16))
DSW_BLOCK = 128
CONV_CH = 512
CONV_WIDTH = 31
SB_HEADS = 8
D_FF = 2816
FFN_CONV = 3
ROPE_THETA = 500000.0
ROPE_DIMS = 16

LANES = 128
SUBLANES = 8
VMEM_LIMIT = 56 * 1024 * 1024
MASK_NEG = -1e30
SIGN_BIT = 0x80000000

PROJ_TM = 512
FFN_TM = 512
FFN_TF = 256
CONV_TS = 512
CONV_HALO = 32
CONV_RB = 32
SB_TILE = 256
DSW_GROUP = 2
DSW_COMBINE_ROWS = 256


def _dot(a, b):
    return jnp.dot(a, b, preferred_element_type=F32)


def _dot_nt(a, b):
    return lax.dot_general(a, b, (((1,), (1,)), ((), ())), preferred_element_type=F32)


def _dot_tn(a, b):
    return lax.dot_general(a, b, (((0,), (0,)), ((), ())), preferred_element_type=F32)


def _rmsnorm(x, g):
    return x * lax.rsqrt(jnp.mean(x * x, axis=-1, keepdims=True) + EPS) * g


def _softplus_neg_abs(z):
    return jnp.log1p(jnp.exp(-jnp.abs(z)))


def _log_sigmoid(z):
    return jnp.minimum(z, 0.0) - _softplus_neg_abs(z)


def _silu(x):
    return x * jax.nn.sigmoid(x)


def _head0_lanes(shape):
    return lax.broadcasted_iota(jnp.int32, shape, len(shape) - 1) < HEAD_DIM


def _resident(shape):
    nd = len(shape)
    return pl.BlockSpec(shape, lambda *_: (0,) * nd, pipeline_mode=pl.Buffered(1))


def _params(*sem):
    return pltpu.CompilerParams(dimension_semantics=sem, vmem_limit_bytes=VMEM_LIMIT)


def _rope(x, cos, sin_lo, sin_hi):
    half = ROPE_DIMS // 2
    return x * cos + pltpu.roll(x, LANES - half, 1) * sin_lo + pltpu.roll(x, half, 1) * sin_hi


def _inproj0_kernel(x_ref, g_ref, w_ref, wa2_ref, ba_ref, cos_ref, sinlo_ref, sinhi_ref,
                    aq_ref, ak_ref, av_ref, ag_ref, la_ref, bq_ref, bk_ref, bv_ref):
    xn = _rmsnorm(x_ref[...], g_ref[...]).astype(BF16)

    def proj(lo, hi):
        return _dot(xn, w_ref[:, lo:hi])

    aq_ref[...] = (proj(0, 256) * (GLA_DK ** -0.5)).astype(BF16)
    ak_ref[...] = proj(256, 512).astype(BF16)
    av_ref[...] = proj(512, 1024).astype(BF16)
    ag_ref[...] = proj(1024, 1536)
    ar = proj(3072, 3200)
    gate_logit = _dot(ar.astype(BF16), wa2_ref[...]) + ba_ref[...]
    la_ref[...] = _log_sigmoid(gate_logit) / GLA_TAU
    cos, sin_lo, sin_hi = cos_ref[...], sinlo_ref[...], sinhi_ref[...]
    bq = proj(1536, 2048)
    bk = proj(2048, 2560)
    for blk in range(4):
        sl = slice(blk * LANES, (blk + 1) * LANES)
        bq_ref[:, sl] = _rope(bq[:, sl], cos, sin_lo, sin_hi) * (HEAD_DIM ** -0.5)
        bk_ref[:, sl] = _rope(bk[:, sl], cos, sin_lo, sin_hi)
    bv_ref[...] = proj(2560, 3072)


def _rope_tables(seq):
    half = ROPE_DIMS // 2
    inv = ROPE_THETA ** (-jnp.arange(half, dtype=F32) / half)
    ang = jnp.arange(seq, dtype=F32)[:, None] * inv[None, :]
    cos, sin = jnp.cos(ang), jnp.sin(ang)
    ones = jnp.ones((seq, HEAD_DIM - ROPE_DIMS), F32)
    zeros = jnp.zeros((seq, HEAD_DIM - ROPE_DIMS), F32)
    zh = jnp.zeros((seq, half), F32)
    cos_h = jnp.concatenate([cos, cos, ones], axis=1)
    lo_h = jnp.concatenate([-sin, zh, zeros], axis=1)
    hi_h = jnp.concatenate([zh, sin, zeros], axis=1)
    two = lambda t: jnp.concatenate([t, t], axis=1)
    return two(cos_h), two(lo_h), two(hi_h)


def _inproj0(x2, g, w_in, wa2, ba, seq):
    T = x2.shape[0]
    tm = PROJ_TM
    aw_k, aw_v, bw = GLA_HEADS * GLA_DK, GLA_HEADS * GLA_DV, DSW_HEADS * HEAD_DIM
    o = [0, aw_k, 2 * aw_k, 2 * aw_k + aw_v, 2 * aw_k + 2 * aw_v]
    o.append(o[-1] + GLA_RANK)
    ar_cols = w_in[:, o[4]:o[5]]
    w = jnp.concatenate(
        [w_in[:, :o[4]], w_in[:, o[5]:], ar_cols, jnp.zeros((D_MODEL, LANES - GLA_RANK), F32)], axis=1
    ).astype(BF16)
    wa2p = jnp.concatenate([wa2, jnp.zeros((LANES - GLA_RANK, aw_k), F32)], axis=0).astype(BF16)
    cos, sin_lo, sin_hi = _rope_tables(seq)
    spt = seq // tm
    row = lambda n: pl.BlockSpec((tm, n), lambda i: (i, 0))
    tab = pl.BlockSpec((tm, LANES), lambda i: (i % spt, 0))
    outs = [(aw_k, BF16), (aw_k, BF16), (aw_v, BF16), (aw_v, F32), (aw_k, F32), (bw, F32), (bw, F32), (bw, F32)]
    return pl.pallas_call(
        _inproj0_kernel,
        grid=(T // tm,),
        in_specs=[row(D_MODEL), _resident((1, D_MODEL)), _resident(w.shape), _resident(wa2p.shape),
                  _resident((1, aw_k)), tab, tab, tab],
        out_specs=[row(n) for n, _ in outs],
        out_shape=[jax.ShapeDtypeStruct((T, n), dt) for n, dt in outs],
        compiler_params=_params("parallel"),
        name="inproj0",
    )(x2, g.reshape(1, -1), w, wa2p, ba.reshape(1, -1), cos, sin_lo, sin_hi)


def _gla_kernel(q_ref, k_ref, la_ref, v_ref, gate_ref, gn_ref, o_ref, st0_ref, st1_ref):
    C, G = GLA_CHUNK, GLA_GROUP
    seq = q_ref.shape[0]
    st_refs = (st0_ref, st1_ref)
    for st in st_refs:
        st[...] = jnp.zeros_like(st)
    head0 = _head0_lanes((C, LANES))
    r = lax.broadcasted_iota(jnp.int32, (C, C), 0)
    c = lax.broadcasted_iota(jnp.int32, (C, C), 1)
    tril = c <= r
    tri = jnp.where(tril, 1.0, 0.0).astype(BF16)
    tri3 = jnp.concatenate([tri, tri, tri], axis=1)
    gn = gn_ref[...]

    def body(it, _):
        rows = [pl.ds(pl.multiple_of((it * G + j) * C, C), C) for j in range(G)]
        la = jnp.concatenate([la_ref[rw, :] for rw in rows], axis=1)
        hi = la.astype(BF16)
        rem = la - hi.astype(F32)
        mid = rem.astype(BF16)
        lo = (rem - mid.astype(F32)).astype(BF16)
        bcum_all = _dot(tri3, jnp.concatenate([hi, mid, lo], axis=0))
        chains, decs = [], []
        for j in range(G):
            bcum = bcum_all[:, j * LANES:(j + 1) * LANES]
            btot = bcum[C - 1:C, :]
            q = q_ref[rows[j], :].astype(F32)
            k = k_ref[rows[j], :].astype(F32)
            q_dec = q * jnp.exp(bcum)
            k_inv = (k * jnp.exp(-bcum)).astype(BF16)
            k_tail = k * jnp.exp(btot - bcum)
            decs.append(jnp.exp(btot))
            for h in range(2):
                lanes_h = head0 if h == 0 else jnp.logical_not(head0)
                qh = jnp.where(lanes_h, q_dec, 0.0).astype(BF16)
                kth = jnp.where(lanes_h, k_tail, 0.0).astype(BF16)
                vh = v_ref[rows[j], h * GLA_DV:(h + 1) * GLA_DV]
                chains.append((j, h, qh, k_inv, kth, vh))
        scores = [jnp.where(tril, _dot_nt(qh, k_inv), 0.0).astype(BF16) for _, _, qh, k_inv, _, _ in chains]
        incs = [_dot_tn(vh, kth) for _, _, _, _, kth, vh in chains]
        states = {}
        for h in range(2):
            st = st_refs[h][...]
            for j in range(G):
                states[j, h] = st.astype(BF16)
                st = st * decs[j] + incs[2 * j + h]
            st_refs[h][...] = st
        outs = [_dot(sc, vh) + _dot_nt(qh, states[j, h])
                for sc, (j, h, qh, _, _, vh) in zip(scores, chains)]
        for o, (j, h, _, _, _, _) in zip(outs, chains):
            cols = slice(h * GLA_DV, (h + 1) * GLA_DV)
            y = _rmsnorm(o, gn) * _silu(gate_ref[rows[j], cols])
            o_ref[rows[j], cols] = y.astype(o_ref.dtype)
        return 0

    lax.fori_loop(0, seq // (C * G), body, 0)


def _gla(aq, ak, la, av, ag, gla_norm, batch, seq):
    npairs = GLA_HEADS // 2
    qk = pl.BlockSpec((seq, LANES), lambda b, p: (b, p))
    vv = pl.BlockSpec((seq, 2 * GLA_DV), lambda b, p: (b, p))
    return pl.pallas_call(
        _gla_kernel,
        grid=(batch, npairs),
        in_specs=[qk, qk, qk, vv, vv, _resident((1, GLA_DV))],
        out_specs=vv,
        out_shape=jax.ShapeDtypeStruct((batch * seq, GLA_HEADS * GLA_DV), BF16),
        scratch_shapes=[pltpu.VMEM((GLA_DV, LANES), F32), pltpu.VMEM((GLA_DV, LANES), F32)],
        compiler_params=_params("parallel", "parallel"),
        name="gla",
    )(aq, ak, la, av, ag, gla_norm.reshape(1, -1))


def _dsw_kernel(q_ref, k_ref, v_ref, o_ref, *stat_refs):
    C, G = DSW_BLOCK, DSW_GROUP
    seq = q_ref.shape[0]
    npat = len(DSW_PATTERNS)
    m_refs, l_refs, a_refs = stat_refs[0:npat], stat_refs[npat:2 * npat], stat_refs[2 * npat:3 * npat]
    head0 = _head0_lanes((C, LANES))
    qi = lax.broadcasted_iota(jnp.int32, (C, 2 * C), 0)
    ki = lax.broadcasted_iota(jnp.int32, (C, 2 * C), 1)
    valid_cur = jnp.logical_and(ki >= C, ki - C <= qi)
    valid_both = jnp.logical_or(jnp.logical_and(ki < C, ki >= qi), valid_cur)

    def rows(start, d):
        return pl.ds(start, C) if d == 1 else pl.ds(start, C, stride=d)

    def score_stage(pat, d, nb, it):
        base = it * G
        if nb > 1:
            assert nb % G == 0
            stream, blk0 = base // nb, base % nb
            starts = [stream + d * C * (blk0 + j) for j in range(G)]
            first = rows(max(starts[0] - d * C, 0), d)
            kb = [k_ref[first, :].astype(BF16)] + [k_ref[rows(s, d), :].astype(BF16) for s in starts]
            vb = [v_ref[first, :].astype(BF16)] + [v_ref[rows(s, d), :].astype(BF16) for s in starts]
            kcats = [jnp.concatenate([kb[j], kb[j + 1]], axis=0) for j in range(G)]
            vcats = [jnp.concatenate([vb[j], vb[j + 1]], axis=0) for j in range(G)]
            valids = [valid_both if blk0 + j > 0 else valid_cur for j in range(G)]
        else:
            starts = [base + j for j in range(G)]
            kcats = [k_ref[rows(s, d), :].astype(BF16) for s in starts]
            vcats = [v_ref[rows(s, d), :].astype(BF16) for s in starts]
            valids = [valid_cur[:, C:]] * G
        scores = []
        for j in range(G):
            q = q_ref[rows(starts[j], d), :]
            for h in range(2):
                lanes_h = head0 if h == 0 else jnp.logical_not(head0)
                scores.append(_dot_nt(jnp.where(lanes_h, q, 0.0).astype(BF16), kcats[j]))
        return pat, d, starts, scores, vcats, valids

    def value_stage(pat, d, starts, scores, vcats, valids):
        ones = jnp.ones((vcats[0].shape[0], LANES), BF16)
        vexts = [jnp.concatenate([v, ones], axis=1) for v in vcats]
        stats = []
        for i, s in enumerate(scores):
            s = jnp.where(valids[i // 2], s, MASK_NEG)
            m = jnp.max(s, axis=-1, keepdims=True)
            stats.append((m, jnp.exp(s - m).astype(BF16)))
        nums = [_dot(p, vexts[i // 2]) for i, (_, p) in enumerate(stats)]
        for j in range(G):
            cur = rows(starts[j], d)
            n0, n1 = nums[2 * j], nums[2 * j + 1]
            m_refs[pat][cur, :] = jnp.where(head0, stats[2 * j][0], stats[2 * j + 1][0])
            l_refs[pat][cur, :] = jnp.where(head0, n0[:, LANES:], n1[:, LANES:])
            a_refs[pat][cur, :] = jnp.where(head0, n0[:, :LANES], n1[:, :LANES])

    groups = []
    for pat, (window, d) in enumerate(DSW_PATTERNS):
        assert window // d == C
        groups += [(pat, d, seq // d // C, it) for it in range(seq // C // G)]
    pending = score_stage(*groups[0])
    for i in range(len(groups)):
        upcoming = score_stage(*groups[i + 1]) if i + 1 < len(groups) else None
        value_stage(*pending)
        pending = upcoming

    R = DSW_COMBINE_ROWS
    for r0 in range(0, seq, R):
        rows_r = slice(r0, r0 + R)
        ms = [m[rows_r, :] for m in m_refs]
        m_max = functools.reduce(jnp.maximum, ms)
        ws = [jnp.exp(m - m_max) for m in ms]
        num = sum(w * a[rows_r, :] for w, a in zip(ws, a_refs))
        den = sum(w * l[rows_r, :] for w, l in zip(ws, l_refs))
        o_ref[rows_r, :] = (num / den).astype(o_ref.dtype)


def _dsw(bq, bk, bv, batch, seq):
    npairs = DSW_HEADS // 2
    spec = pl.BlockSpec((seq, LANES), lambda b, p: (b, p))
    return pl.pallas_call(
        _dsw_kernel,
        grid=(batch, npairs),
        in_specs=[spec, spec, spec],
        out_specs=spec,
        out_shape=jax.ShapeDtypeStruct((batch * seq, DSW_HEADS * HEAD_DIM), BF16),
        scratch_shapes=[pltpu.VMEM((seq, LANES), F32) for _ in range(3 * len(DSW_PATTERNS))],
        compiler_params=_params("parallel", "parallel"),
        name="dsw",
    )(bq, bk, bv)


def _ffn_kernel(h_ref, ma_ref, mb_ref, wo_ref, gf_ref, wg_ref, wv_ref, cg_ref, cv_ref, wd_ref, gfin_ref,
                o_ref, carry_ref, ubuf_ref, xn_ref, act_ref, *, tiles_per_seq, final_norm):
    tm = h_ref.shape[0]
    half = ma_ref.shape[1]
    nchunks, _, tf = wg_ref.shape
    H = SUBLANES

    @pl.when(pl.program_id(0) % tiles_per_seq == 0)
    def _():
        carry_ref[...] = jnp.zeros_like(carry_ref)

    h1 = h_ref[...] + _dot(ma_ref[...], wo_ref[0:half, :]) + _dot(mb_ref[...], wo_ref[half:, :])
    xn_ref[...] = _rmsnorm(h1, gf_ref[...]).astype(BF16)
    o_ref[...] = h1

    def up(c):
        for slot, w_ref in enumerate((wg_ref, wv_ref)):
            buf = ubuf_ref.at[c % 2, slot]
            buf[0:H, :] = carry_ref[c, slot]
            buf[H:, :] = _dot(xn_ref[...], w_ref[c])
            carry_ref[c, slot] = buf[tm:tm + H, :]

    def conv(c, slot, w):
        buf = ubuf_ref.at[c % 2, slot]
        out = w[FFN_CONV - 1:FFN_CONV, :] * buf[H:, :]
        for j in range(1, FFN_CONV):
            out = out + w[FFN_CONV - 1 - j:FFN_CONV - j, :] * buf[H - j:H - j + tm, :]
        return out

    up(0)
    for c in range(nchunks):
        if c + 1 < nchunks:
            up(c + 1)
        act = _silu(conv(c, 0, cg_ref[c])) * conv(c, 1, cv_ref[c])
        act_ref[:, c * tf:(c + 1) * tf] = act.astype(BF16)
    out = o_ref[...] + _dot(act_ref[...], wd_ref[...])
    if final_norm:
        out = _rmsnorm(out, gfin_ref[...])
    o_ref[...] = out


def _outproj_ffn(h2, mix_a, mix_b, w_out, g_ffn, w_up, w_conv, w_down, g_final, seq, final_norm):
    T = h2.shape[0]
    tm, tf = FFN_TM, FFN_TF
    nchunks = D_FF // tf
    half = mix_a.shape[1]
    chunked_cols = lambda w: w.reshape(w.shape[0], nchunks, tf).transpose(1, 0, 2)
    wg = chunked_cols(w_up[:, :D_FF]).astype(BF16)
    wv = chunked_cols(w_up[:, D_FF:]).astype(BF16)
    pad = jnp.zeros((SUBLANES - FFN_CONV, 2 * D_FF), F32)
    wc = jnp.concatenate([w_conv, pad], axis=0)
    cg = chunked_cols(wc[:, :D_FF])
    cv = chunked_cols(wc[:, D_FF:])
    wd = w_down.astype(BF16)
    row = lambda n: pl.BlockSpec((tm, n), lambda i: (i, 0))
    kern = functools.partial(_ffn_kernel, tiles_per_seq=seq // tm, final_norm=final_norm)
    return pl.pallas_call(
        kern,
        grid=(T // tm,),
        in_specs=[row(D_MODEL), row(half), row(half), _resident((2 * half, D_MODEL)), _resident((1, D_MODEL)),
                  _resident(wg.shape), _resident(wv.shape), _resident(cg.shape), _resident(cv.shape),
                  _resident(wd.shape), _resident((1, D_MODEL))],
        out_specs=row(D_MODEL),
        out_shape=jax.ShapeDtypeStruct((T, D_MODEL), F32),
        scratch_shapes=[pltpu.VMEM((nchunks, 2, SUBLANES, tf), F32),
                        pltpu.VMEM((2, 2, tm + SUBLANES, tf), F32),
                        pltpu.VMEM((tm, D_MODEL), BF16),
                        pltpu.VMEM((tm, D_FF), BF16)],
        compiler_params=_params("arbitrary"),
        name="outproj_ffn",
    )(h2, mix_a, mix_b, w_out.astype(BF16), g_ffn.reshape(1, -1), wg, wv, cg, cv, wd, g_final.reshape(1, -1))


def _inproj1_kernel(x_ref, g_ref, w_ref, c_ref, q_ref, k_ref, v_ref):
    xn = _rmsnorm(x_ref[...], g_ref[...]).astype(BF16)

    def proj(lo, hi):
        return _dot(xn, w_ref[:, lo:hi])

    c_ref[...] = proj(0, 512) * jax.nn.sigmoid(proj(512, 1024))
    q_ref[...] = (proj(1024, 1536) * (HEAD_DIM ** -0.5)).astype(BF16)
    k_ref[...] = proj(1536, 2048).astype(BF16)
    v_ref[...] = proj(2048, 2560).astype(BF16)


def _inproj1(x2, g, w_in):
    T = x2.shape[0]
    tm = PROJ_TM
    row = lambda n: pl.BlockSpec((tm, n), lambda i: (i, 0))
    outs = [(CONV_CH, F32), (512, BF16), (512, BF16), (512, BF16)]
    return pl.pallas_call(
        _inproj1_kernel,
        grid=(T // tm,),
        in_specs=[row(D_MODEL), _resident((1, D_MODEL)), _resident(w_in.shape)],
        out_specs=[row(n) for n, _ in outs],
        out_shape=[jax.ShapeDtypeStruct((T, n), dt) for n, dt in outs],
        compiler_params=_params("parallel"),
        name="inproj1",
    )(x2, g.reshape(1, -1), w_in.astype(BF16))


def _convmod_kernel(c_ref, w_ref, b_ref, lg_ref, lb_ref, o_ref, buf_ref, sh_ref):
    ts = c_ref.shape[0]
    HALO = CONV_HALO

    @pl.when(pl.program_id(1) == 0)
    def _():
        buf_ref[0:HALO, :] = jnp.zeros((HALO, CONV_CH), F32)

    @pl.when(pl.program_id(1) > 0)
    def _():
        buf_ref[0:HALO, :] = buf_ref[ts:ts + HALO, :]

    buf_ref[HALO:, :] = c_ref[...]
    n_sh = ts + HALO - SUBLANES
    for b in range(1, SUBLANES):
        sh_ref[b, 0:n_sh, :] = buf_ref[b:b + n_sh, :]
    bias, lg, lb = b_ref[...], lg_ref[...], lb_ref[...]
    R = CONV_RB

    def body(rb, _):
        r0 = pl.multiple_of(rb * R, R)
        accs = [jnp.zeros((R // SUBLANES, SUBLANES, CONV_CH), F32) for _ in range(2)]
        for kk in range(CONV_WIDTH):
            off = HALO - (CONV_WIDTH - 1) + kk
            a, b = off // SUBLANES, off % SUBLANES
            rows = pl.ds(r0 + a * SUBLANES, R)
            x = buf_ref[rows, :] if b == 0 else sh_ref[b, rows, :]
            accs[kk % 2] = accs[kk % 2] + w_ref[kk][None] * x.reshape(R // SUBLANES, SUBLANES, CONV_CH)
        acc = (accs[0] + accs[1]).reshape(R, CONV_CH)
        y = acc + bias
        mu = jnp.mean(y, axis=-1, keepdims=True)
        yc = y - mu
        var = jnp.mean(yc * yc, axis=-1, keepdims=True)
        z = yc * lax.rsqrt(var + EPS) * lg + lb
        o_ref[pl.ds(r0, R), :] = _silu(z).astype(o_ref.dtype)
        return 0

    lax.fori_loop(0, ts // R, body, 0, unroll=2)


def _convmod(c, conv_w, conv_b, ln_g, ln_b, batch, seq):
    ts = CONV_TS
    spb = seq // ts
    wrep = jnp.broadcast_to(conv_w[:, None, :], (CONV_WIDTH, SUBLANES, CONV_CH))
    spec = pl.BlockSpec((ts, CONV_CH), lambda b, s: (b * spb + s, 0))
    vec = _resident((1, CONV_CH))
    return pl.pallas_call(
        _convmod_kernel,
        grid=(batch, spb),
        in_specs=[spec, _resident(wrep.shape), vec, vec, vec],
        out_specs=spec,
        out_shape=jax.ShapeDtypeStruct((batch * seq, CONV_CH), BF16),
        scratch_shapes=[pltpu.VMEM((CONV_HALO + ts, CONV_CH), F32),
                        pltpu.VMEM((SUBLANES, CONV_HALO + ts, CONV_CH), F32)],
        compiler_params=_params("parallel", "arbitrary"),
        name="convmod",
    )(c, wrep, conv_b.reshape(1, -1), ln_g.reshape(1, -1), ln_b.reshape(1, -1))


def _sb_kernel(q_ref, k_ref, v_ref, u_ref, o_ref, carry_ref, acc_ref):
    T = SB_TILE
    nblk = q_ref.shape[0] // T
    head0 = _head0_lanes((T, LANES))
    qpos = lax.broadcasted_iota(jnp.int32, (T, T), 0)
    kpos = lax.broadcasted_iota(jnp.int32, (T, T), 1)
    below_diag = kpos < qpos
    carry_ref[...] = jnp.zeros_like(carry_ref)
    acc_ref[...] = jnp.zeros_like(acc_ref)

    def score_stage(kb):
        rows = slice(kb * T, (kb + 1) * T)
        k_blk = k_ref[rows, :]
        chains = [(qb, h, qb == kb) for qb in range(kb, nblk) for h in range(2)]
        zs = []
        for qb, h, _ in chains:
            q = q_ref[qb * T:(qb + 1) * T, :]
            lanes_h = head0 if h == 0 else jnp.logical_not(head0)
            zs.append(_dot_nt(jnp.where(lanes_h, q, jnp.zeros_like(q)), k_blk))
        return chains, zs, v_ref[rows, :]

    def value_stage(chains, zs, v_blk):
        nls = []
        for (qb, h, masked), z in zip(chains, zs):
            neg_abs = lax.bitcast_convert_type(lax.bitcast_convert_type(z, jnp.uint32) | jnp.uint32(SIGN_BIT), F32)
            nl = jnp.maximum(z, 0.0) + jnp.log(1.0 + jnp.exp(neg_abs))
            if masked:
                nl = jnp.where(below_diag, nl, 0.0)
            nls.append(nl.astype(BF16))
        incls = [_dot(nl, u_ref[...]) for nl in nls]
        probs = []
        for (qb, h, masked), z, incl in zip(chains, zs, incls):
            carry = carry_ref[qb, h]
            a = jnp.exp(z - (incl + jnp.concatenate([carry] * (T // LANES), axis=1)))
            if masked:
                a = jnp.where(below_diag, a, 0.0)
            carry_ref[qb, h] = carry + jnp.broadcast_to(incl[:, 0:1], (T, LANES))
            probs.append(a.astype(BF16))
        for (qb, h, _), a in zip(chains, probs):
            acc_ref[qb, h] += _dot(a, v_blk)

    pending = score_stage(nblk - 1)
    for kb in reversed(range(nblk)):
        upcoming = score_stage(kb - 1) if kb > 0 else None
        value_stage(*pending)
        pending = upcoming
    for qb in range(nblk):
        out = jnp.where(head0, acc_ref[qb, 0], acc_ref[qb, 1])
        o_ref[qb * T:(qb + 1) * T, :] = out.astype(o_ref.dtype)


def _sb(dq, dk, dv, batch, seq):
    T = SB_TILE
    npairs = SB_HEADS // 2
    nblk = seq // T
    j = lax.broadcasted_iota(jnp.int32, (T, T), 0)
    s = lax.broadcasted_iota(jnp.int32, (T, T), 1)
    u = jnp.where(j >= s, 1.0, 0.0).astype(BF16)
    spec = pl.BlockSpec((seq, LANES), lambda b, p: (b, p))
    return pl.pallas_call(
        _sb_kernel,
        grid=(batch, npairs),
        in_specs=[spec, spec, spec, _resident((T, T))],
        out_specs=spec,
        out_shape=jax.ShapeDtypeStruct((batch * seq, SB_HEADS * HEAD_DIM), BF16),
        scratch_shapes=[pltpu.VMEM((nblk, 2, T, LANES), F32), pltpu.VMEM((nblk, 2, T, LANES), F32)],
        compiler_params=_params("parallel", "parallel"),
        name="stickbreak",
    )(dq, dk, dv, u)


def kernel(x, norm_mix0, w_in0, gla_wa2, gla_ba, gla_norm, w_out0, norm_ffn0, ffn_up0, ffn_conv0, ffn_down0,
           norm_mix1, w_in1, conv_w1, conv_b1, conv_ln_g1, conv_ln_b1, w_out1, norm_ffn1, ffn_up1, ffn_conv1,
           ffn_down1, final_norm):
    batch, seq, d = x.shape
    h = x.reshape(batch * seq, d)
    aq, ak, av, ag, la, bq, bk, bv = _inproj0(h, norm_mix0, w_in0, gla_wa2, gla_ba, seq)
    oa = _gla(aq, ak, la, av, ag, gla_norm, batch, seq)
    ob = _dsw(bq, bk, bv, batch, seq)
    h = _outproj_ffn(h, oa, ob, w_out0, norm_ffn0, ffn_up0, ffn_conv0, ffn_down0, final_norm, seq, False)
    c, dq, dk, dv = _inproj1(h, norm_mix1, w_in1)
    oc = _convmod(c, conv_w1, conv_b1, conv_ln_g1, conv_ln_b1, batch, seq)
    od = _sb(dq, dk, dv, batch, seq)
    h = _outproj_ffn(h, oc, od, w_out1, norm_ffn1, ffn_up1, ffn_conv1, ffn_down1, final_norm, seq, True)
    return h.reshape(batch, seq, d)
```

```python
import functools

import jax
import jax.numpy as jnp
from jax import lax
from jax.experimental import pallas as pl
from jax.experimental.pallas import tpu as pltpu

F32 = jnp.float32
BF16 = jnp.bfloat16

D_MODEL = 1024
HEAD_DIM = 64
EPS = 1e-6
GLA_HEADS = 4
GLA_DV = 128
GLA_DK = 64
GLA_RANK = 16
GLA_TAU = 16.0
GLA_CHUNK = 64
GLA_GROUP = 32
DSW_HEADS = 8
DSW_PATTERNS = ((128, 1), (512, 4), (2048, 16))
DSW_BLOCK = 128
CONV_CH = 512
CONV_WIDTH = 31
SB_HEADS = 8
D_FF = 2816
FFN_CONV = 3
ROPE_THETA = 500000.0
ROPE_DIMS = 16

LANES = 128
SUBLANES = 8
VMEM_LIMIT = 56 * 1024 * 1024
MASK_NEG = -1e30
SIGN_BIT = 0x80000000

PROJ_TM = 512
FFN_TM = 512
FFN_TF = 256
CONV_TS = 512
CONV_HALO = 32
CONV_RB = 32
SB_TILE = 256
DSW_GROUP = 2
DSW_COMBINE_ROWS = 256


def _dot(a, b):
    return jnp.dot(a, b, preferred_element_type=F32)


def _dot_nt(a, b):
    return lax.dot_general(a, b, (((1,), (1,)), ((), ())), preferred_element_type=F32)


def _dot_tn(a, b):
    return lax.dot_general(a, b, (((0,), (0,)), ((), ())), preferred_element_type=F32)


def _rmsnorm(x, g):
    return x * lax.rsqrt(jnp.mean(x * x, axis=-1, keepdims=True) + EPS) * g


def _softplus_neg_abs(z):
    return jnp.log1p(jnp.exp(-jnp.abs(z)))


def _log_sigmoid(z):
    return jnp.minimum(z, 0.0) - _softplus_neg_abs(z)


def _silu(x):
    return x * jax.nn.sigmoid(x)


def _head0_lanes(shape):
    return lax.broadcasted_iota(jnp.int32, shape, len(shape) - 1) < HEAD_DIM


def _resident(shape):
    nd = len(shape)
    return pl.BlockSpec(shape, lambda *_: (0,) * nd, pipeline_mode=pl.Buffered(1))


def _params(*sem):
    return pltpu.CompilerParams(dimension_semantics=sem, vmem_limit_bytes=VMEM_LIMIT)


def _rope(x, cos, sin_lo, sin_hi):
    half = ROPE_DIMS // 2
    return x * cos + pltpu.roll(x, LANES - half, 1) * sin_lo + pltpu.roll(x, half, 1) * sin_hi


def _inproj0_kernel(x_ref, g_ref, w_ref, wa2_ref, ba_ref, cos_ref, sinlo_ref, sinhi_ref,
                    aq_ref, ak_ref, av_ref, ag_ref, la_ref, bq_ref, bk_ref, bv_ref):
    xn = _rmsnorm(x_ref[...], g_ref[...]).astype(BF16)

    def proj(lo, hi):
        return _dot(xn, w_ref[:, lo:hi])

    aq_ref[...] = (proj(0, 256) * (GLA_DK ** -0.5)).astype(BF16)
    ak_ref[...] = proj(256, 512).astype(BF16)
    av_ref[...] = proj(512, 1024).astype(BF16)
    ag_ref[...] = proj(1024, 1536)
    ar = proj(3072, 3200)
    gate_logit = _dot(ar.astype(BF16), wa2_ref[...]) + ba_ref[...]
    la_ref[...] = _log_sigmoid(gate_logit) / GLA_TAU
    cos, sin_lo, sin_hi = cos_ref[...], sinlo_ref[...], sinhi_ref[...]
    bq = proj(1536, 2048)
    bk = proj(2048, 2560)
    for blk in range(4):
        sl = slice(blk * LANES, (blk + 1) * LANES)
        bq_ref[:, sl] = _rope(bq[:, sl], cos, sin_lo, sin_hi) * (HEAD_DIM ** -0.5)
        bk_ref[:, sl] = _rope(bk[:, sl], cos, sin_lo, sin_hi)
    bv_ref[...] = proj(2560, 3072)


def _rope_tables(seq):
    half = ROPE_DIMS // 2
    inv = ROPE_THETA ** (-jnp.arange(half, dtype=F32) / half)
    ang = jnp.arange(seq, dtype=F32)[:, None] * inv[None, :]
    cos, sin = jnp.cos(ang), jnp.sin(ang)
    ones = jnp.ones((seq, HEAD_DIM - ROPE_DIMS), F32)
    zeros = jnp.zeros((seq, HEAD_DIM - ROPE_DIMS), F32)
    zh = jnp.zeros((seq, half), F32)
    cos_h = jnp.concatenate([cos, cos, ones], axis=1)
    lo_h = jnp.concatenate([-sin, zh, zeros], axis=1)
    hi_h = jnp.concatenate([zh, sin, zeros], axis=1)
    two = lambda t: jnp.concatenate([t, t], axis=1)
    return two(cos_h), two(lo_h), two(hi_h)


def _inproj0(x2, g, w_in, wa2, ba, seq):
    T = x2.shape[0]
    tm = PROJ_TM
    aw_k, aw_v, bw = GLA_HEADS * GLA_DK, GLA_HEADS * GLA_DV, DSW_HEADS * HEAD_DIM
    o = [0, aw_k, 2 * aw_k, 2 * aw_k + aw_v, 2 * aw_k + 2 * aw_v]
    o.append(o[-1] + GLA_RANK)
    ar_cols = w_in[:, o[4]:o[5]]
    w = jnp.concatenate(
        [w_in[:, :o[4]], w_in[:, o[5]:], ar_cols, jnp.zeros((D_MODEL, LANES - GLA_RANK), F32)], axis=1
    ).astype(BF16)
    wa2p = jnp.concatenate([wa2, jnp.zeros((LANES - GLA_RANK, aw_k), F32)], axis=0).astype(BF16)
    cos, sin_lo, sin_hi = _rope_tables(seq)
    spt = seq // tm
    row = lambda n: pl.BlockSpec((tm, n), lambda i: (i, 0))
    tab = pl.BlockSpec((tm, LANES), lambda i: (i % spt, 0))
    outs = [(aw_k, BF16), (aw_k, BF16), (aw_v, BF16), (aw_v, F32), (aw_k, F32), (bw, F32), (bw, F32), (bw, F32)]
    return pl.pallas_call(
        _inproj0_kernel,
        grid=(T // tm,),
        in_specs=[row(D_MODEL), _resident((1, D_MODEL)), _resident(w.shape), _resident(wa2p.shape),
                  _resident((1, aw_k)), tab, tab, tab],
        out_specs=[row(n) for n, _ in outs],
        out_shape=[jax.ShapeDtypeStruct((T, n), dt) for n, dt in outs],
        compiler_params=_params("parallel"),
        name="inproj0",
    )(x2, g.reshape(1, -1), w, wa2p, ba.reshape(1, -1), cos, sin_lo, sin_hi)


def _gla_kernel(q_ref, k_ref, la_ref, v_ref, gate_ref, gn_ref, o_ref, st0_ref, st1_ref):
    C, G = GLA_CHUNK, GLA_GROUP
    seq = q_ref.shape[0]
    st_refs = (st0_ref, st1_ref)
    for st in st_refs:
        st[...] = jnp.zeros_like(st)
    head0 = _head0_lanes((C, LANES))
    r = lax.broadcasted_iota(jnp.int32, (C, C), 0)
    c = lax.broadcasted_iota(jnp.int32, (C, C), 1)
    tril = c <= r
    tri = jnp.where(tril, 1.0, 0.0).astype(BF16)
    tri3 = jnp.concatenate([tri, tri, tri], axis=1)
    gn = gn_ref[...]

    def body(it, _):
        rows = [pl.ds(pl.multiple_of((it * G + j) * C, C), C) for j in range(G)]
        la = jnp.concatenate([la_ref[rw, :] for rw in rows], axis=1)
        hi = la.astype(BF16)
        rem = la - hi.astype(F32)
        mid = rem.astype(BF16)
        lo = (rem - mid.astype(F32)).astype(BF16)
        bcum_all = _dot(tri3, jnp.concatenate([hi, mid, lo], axis=0))
        chains, decs = [], []
        for j in range(G):
            bcum = bcum_all[:, j * LANES:(j + 1) * LANES]
            btot = bcum[C - 1:C, :]
            q = q_ref[rows[j], :].astype(F32)
            k = k_ref[rows[j], :].astype(F32)
            q_dec = q * jnp.exp(bcum)
            k_inv = (k * jnp.exp(-bcum)).astype(BF16)
            k_tail = k * jnp.exp(btot - bcum)
            decs.append(jnp.exp(btot))
            for h in range(2):
                lanes_h = head0 if h == 0 else jnp.logical_not(head0)
                qh = jnp.where(lanes_h, q_dec, 0.0).astype(BF16)
                kth = jnp.where(lanes_h, k_tail, 0.0).astype(BF16)
                vh = v_ref[rows[j], h * GLA_DV:(h + 1) * GLA_DV]
                chains.append((j, h, qh, k_inv, kth, vh))
        scores = [jnp.where(tril, _dot_nt(qh, k_inv), 0.0).astype(BF16) for _, _, qh, k_inv, _, _ in chains]
        incs = [_dot_tn(vh, kth) for _, _, _, _, kth, vh in chains]
        states = {}
        for h in range(2):
            st = st_refs[h][...]
            for j in range(G):
                states[j, h] = st.astype(BF16)
                st = st * decs[j] + incs[2 * j + h]
            st_refs[h][...] = st
        outs = [_dot(sc, vh) + _dot_nt(qh, states[j, h])
                for sc, (j, h, qh, _, _, vh) in zip(scores, chains)]
        for o, (j, h, _, _, _, _) in zip(outs, chains):
            cols = slice(h * GLA_DV, (h + 1) * GLA_DV)
            y = _rmsnorm(o, gn) * _silu(gate_ref[rows[j], cols])
            o_ref[rows[j], cols] = y.astype(o_ref.dtype)
        return 0

    lax.fori_loop(0, seq // (C * G), body, 0)


def _gla(aq, ak, la, av, ag, gla_norm, batch, seq):
    npairs = GLA_HEADS // 2
    qk = pl.BlockSpec((seq, LANES), lambda b, p: (b, p))
    vv = pl.BlockSpec((seq, 2 * GLA_DV), lambda b, p: (b, p))
    return pl.pallas_call(
        _gla_kernel,
        grid=(batch, npairs),
        in_specs=[qk, qk, qk, vv, vv, _resident((1, GLA_DV))],
        out_specs=vv,
        out_shape=jax.ShapeDtypeStruct((batch * seq, GLA_HEADS * GLA_DV), BF16),
        scratch_shapes=[pltpu.VMEM((GLA_DV, LANES), F32), pltpu.VMEM((GLA_DV, LANES), F32)],
        compiler_params=_params("parallel", "parallel"),
        name="gla",
    )(aq, ak, la, av, ag, gla_norm.reshape(1, -1))


def _dsw_kernel(q_ref, k_ref, v_ref, o_ref, *stat_refs):
    C, G = DSW_BLOCK, DSW_GROUP
    seq = q_ref.shape[0]
    npat = len(DSW_PATTERNS)
    m_refs, l_refs, a_refs = stat_refs[0:npat], stat_refs[npat:2 * npat], stat_refs[2 * npat:3 * npat]
    head0 = _head0_lanes((C, LANES))
    qi = lax.broadcasted_iota(jnp.int32, (C, 2 * C), 0)
    ki = lax.broadcasted_iota(jnp.int32, (C, 2 * C), 1)
    valid_cur = jnp.logical_and(ki >= C, ki - C <= qi)
    valid_both = jnp.logical_or(jnp.logical_and(ki < C, ki >= qi), valid_cur)

    def rows(start, d):
        return pl.ds(start, C) if d == 1 else pl.ds(start, C, stride=d)

    def score_stage(pat, d, nb, it):
        base = it * G
        if nb > 1:
            assert nb % G == 0
            stream, blk0 = base // nb, base % nb
            starts = [stream + d * C * (blk0 + j) for j in range(G)]
            first = rows(max(starts[0] - d * C, 0), d)
            kb = [k_ref[first, :].astype(BF16)] + [k_ref[rows(s, d), :].astype(BF16) for s in starts]
            vb = [v_ref[first, :].astype(BF16)] + [v_ref[rows(s, d), :].astype(BF16) for s in starts]
            kcats = [jnp.concatenate([kb[j], kb[j + 1]], axis=0) for j in range(G)]
            vcats = [jnp.concatenate([vb[j], vb[j + 1]], axis=0) for j in range(G)]
            valids = [valid_both if blk0 + j > 0 else valid_cur for j in range(G)]
        else:
            starts = [base + j for j in range(G)]
            kcats = [k_ref[rows(s, d), :].astype(BF16) for s in starts]
            vcats = [v_ref[rows(s, d), :].astype(BF16) for s in starts]
            valids = [valid_cur[:, C:]] * G
        scores = []
        for j in range(G):
            q = q_ref[rows(starts[j], d), :]
            for h in range(2):
                lanes_h = head0 if h == 0 else jnp.logical_not(head0)
                scores.append(_dot_nt(jnp.where(lanes_h, q, 0.0).astype(BF16), kcats[j]))
        return pat, d, starts, scores, vcats, valids

    def value_stage(pat, d, starts, scores, vcats, valids):
        ones = jnp.ones((vcats[0].shape[0], LANES), BF16)
        vexts = [jnp.concatenate([v, ones], axis=1) for v in vcats]
        stats = []
        for i, s in enumerate(scores):
            s = jnp.where(valids[i // 2], s, MASK_NEG)
            m = jnp.max(s, axis=-1, keepdims=True)
            stats.append((m, jnp.exp(s - m).astype(BF16)))
        nums = [_dot(p, vexts[i // 2]) for i, (_, p) in enumerate(stats)]
        for j in range(G):
            cur = rows(starts[j], d)
            n0, n1 = nums[2 * j], nums[2 * j + 1]
            m_refs[pat][cur, :] = jnp.where(head0, stats[2 * j][0], stats[2 * j + 1][0])
            l_refs[pat][cur, :] = jnp.where(head0, n0[:, LANES:], n1[:, LANES:])
            a_refs[pat][cur, :] = jnp.where(head0, n0[:, :LANES], n1[:, :LANES])

    groups = []
    for pat, (window, d) in enumerate(DSW_PATTERNS):
        assert window // d == C
        groups += [(pat, d, seq // d // C, it) for it in range(seq // C // G)]
    pending = score_stage(*groups[0])
    for i in range(len(groups)):
        upcoming = score_stage(*groups[i + 1]) if i + 1 < len(groups) else None
        value_stage(*pending)
        pending = upcoming

    R = DSW_COMBINE_ROWS
    for r0 in range(0, seq, R):
        rows_r = slice(r0, r0 + R)
        ms = [m[rows_r, :] for m in m_refs]
        m_max = functools.reduce(jnp.maximum, ms)
        ws = [jnp.exp(m - m_max) for m in ms]
        num = sum(w * a[rows_r, :] for w, a in zip(ws, a_refs))
        den = sum(w * l[rows_r, :] for w, l in zip(ws, l_refs))
        o_ref[rows_r, :] = (num / den).astype(o_ref.dtype)


def _dsw(bq, bk, bv, batch, seq):
    npairs = DSW_HEADS // 2
    spec = pl.BlockSpec((seq, LANES), lambda b, p: (b, p))
    return pl.pallas_call(
        _dsw_kernel,
        grid=(batch, npairs),
        in_specs=[spec, spec, spec],
        out_specs=spec,
        out_shape=jax.ShapeDtypeStruct((batch * seq, DSW_HEADS * HEAD_DIM), BF16),
        scratch_shapes=[pltpu.VMEM((seq, LANES), F32) for _ in range(3 * len(DSW_PATTERNS))],
        compiler_params=_params("parallel", "parallel"),
        name="dsw",
    )(bq, bk, bv)


def _ffn_kernel(h_ref, ma_ref, mb_ref, wo_ref, gf_ref, wu_ref, wc_ref, wd_ref, gfin_ref,
                o_ref, carry_ref, ubuf_ref, xn_ref, act_ref, *, tiles_per_seq, final_norm):
    tm = h_ref.shape[0]
    half = ma_ref.shape[1]
    tf = FFN_TF
    nchunks = D_FF // tf
    H = SUBLANES

    @pl.when(pl.program_id(0) % tiles_per_seq == 0)
    def _():
        carry_ref[...] = jnp.zeros_like(carry_ref)

    h1 = h_ref[...] + _dot(ma_ref[...], wo_ref[0:half, :]) + _dot(mb_ref[...], wo_ref[half:, :])
    xn_ref[...] = _rmsnorm(h1, gf_ref[...]).astype(BF16)
    o_ref[...] = h1

    def up(c):
        for slot in range(2):
            buf = ubuf_ref.at[c % 2, slot]
            buf[0:H, :] = carry_ref[c, slot]
            buf[H:, :] = _dot(xn_ref[...], wu_ref[:, slot * D_FF + c * tf:slot * D_FF + (c + 1) * tf])
            carry_ref[c, slot] = buf[tm:tm + H, :]

    def conv(c, slot):
        buf = ubuf_ref.at[c % 2, slot]
        w = wc_ref[:, slot * D_FF + c * tf:slot * D_FF + (c + 1) * tf]
        out = w[FFN_CONV - 1:FFN_CONV, :] * buf[H:, :]
        for j in range(1, FFN_CONV):
            out = out + w[FFN_CONV - 1 - j:FFN_CONV - j, :] * buf[H - j:H - j + tm, :]
        return out

    up(0)
    for c in range(nchunks):
        if c + 1 < nchunks:
            up(c + 1)
        act_ref[:, c * tf:(c + 1) * tf] = (_silu(conv(c, 0)) * conv(c, 1)).astype(BF16)
    out = o_ref[...] + _dot(act_ref[...], wd_ref[...])
    if final_norm:
        out = _rmsnorm(out, gfin_ref[...])
    o_ref[...] = out


def _outproj_ffn(h2, mix_a, mix_b, w_out, g_ffn, w_up, w_conv, w_down, g_final, seq, final_norm):
    T = h2.shape[0]
    tm, tf = FFN_TM, FFN_TF
    nchunks = D_FF // tf
    half = mix_a.shape[1]
    wc = jnp.concatenate([w_conv, jnp.zeros((SUBLANES - FFN_CONV, 2 * D_FF), F32)], axis=0)
    wd = w_down.astype(BF16)
    row = lambda n: pl.BlockSpec((tm, n), lambda i: (i, 0))
    kern = functools.partial(_ffn_kernel, tiles_per_seq=seq // tm, final_norm=final_norm)
    return pl.pallas_call(
        kern,
        grid=(T // tm,),
        in_specs=[row(D_MODEL), row(half), row(half), _resident((2 * half, D_MODEL)), _resident((1, D_MODEL)),
                  _resident(w_up.shape), _resident(wc.shape), _resident(wd.shape), _resident((1, D_MODEL))],
        out_specs=row(D_MODEL),
        out_shape=jax.ShapeDtypeStruct((T, D_MODEL), F32),
        scratch_shapes=[pltpu.VMEM((nchunks, 2, SUBLANES, tf), F32),
                        pltpu.VMEM((2, 2, tm + SUBLANES, tf), F32),
                        pltpu.VMEM((tm, D_MODEL), BF16),
                        pltpu.VMEM((tm, D_FF), BF16)],
        compiler_params=_params("arbitrary"),
        name="outproj_ffn",
    )(h2, mix_a, mix_b, w_out.astype(BF16), g_ffn.reshape(1, -1), w_up.astype(BF16), wc, wd,
      g_final.reshape(1, -1))


def _inproj1_kernel(x_ref, g_ref, w_ref, c_ref, q_ref, k_ref, v_ref):
    xn = _rmsnorm(x_ref[...], g_ref[...]).astype(BF16)

    def proj(lo, hi):
        return _dot(xn, w_ref[:, lo:hi])

    c_ref[...] = proj(0, 512) * jax.nn.sigmoid(proj(512, 1024))
    q_ref[...] = (proj(1024, 1536) * (HEAD_DIM ** -0.5)).astype(BF16)
    k_ref[...] = proj(1536, 2048).astype(BF16)
    v_ref[...] = proj(2048, 2560).astype(BF16)


def _inproj1(x2, g, w_in):
    T = x2.shape[0]
    tm = PROJ_TM
    row = lambda n: pl.BlockSpec((tm, n), lambda i: (i, 0))
    outs = [(CONV_CH, F32), (512, BF16), (512, BF16), (512, BF16)]
    return pl.pallas_call(
        _inproj1_kernel,
        grid=(T // tm,),
        in_specs=[row(D_MODEL), _resident((1, D_MODEL)), _resident(w_in.shape)],
        out_specs=[row(n) for n, _ in outs],
        out_shape=[jax.ShapeDtypeStruct((T, n), dt) for n, dt in outs],
        compiler_params=_params("parallel"),
        name="inproj1",
    )(x2, g.reshape(1, -1), w_in.astype(BF16))


def _convmod_kernel(c_ref, w_ref, b_ref, lg_ref, lb_ref, o_ref, buf_ref, sh_ref):
    ts = c_ref.shape[0]
    HALO = CONV_HALO

    @pl.when(pl.program_id(1) == 0)
    def _():
        buf_ref[0:HALO, :] = jnp.zeros((HALO, CONV_CH), F32)

    @pl.when(pl.program_id(1) > 0)
    def _():
        buf_ref[0:HALO, :] = buf_ref[ts:ts + HALO, :]

    buf_ref[HALO:, :] = c_ref[...]
    n_sh = ts + HALO - SUBLANES
    for b in range(1, SUBLANES):
        sh_ref[b, 0:n_sh, :] = buf_ref[b:b + n_sh, :]
    bias, lg, lb = b_ref[...], lg_ref[...], lb_ref[...]
    R = CONV_RB

    def body(rb, _):
        r0 = pl.multiple_of(rb * R, R)
        accs = [jnp.zeros((R // SUBLANES, SUBLANES, CONV_CH), F32) for _ in range(2)]
        for kk in range(CONV_WIDTH):
            off = HALO - (CONV_WIDTH - 1) + kk
            a, b = off // SUBLANES, off % SUBLANES
            rows = pl.ds(r0 + a * SUBLANES, R)
            x = buf_ref[rows, :] if b == 0 else sh_ref[b, rows, :]
            accs[kk % 2] = accs[kk % 2] + w_ref[kk][None] * x.reshape(R // SUBLANES, SUBLANES, CONV_CH)
        acc = (accs[0] + accs[1]).reshape(R, CONV_CH)
        y = acc + bias
        mu = jnp.mean(y, axis=-1, keepdims=True)
        yc = y - mu
        var = jnp.mean(yc * yc, axis=-1, keepdims=True)
        z = yc * lax.rsqrt(var + EPS) * lg + lb
        o_ref[pl.ds(r0, R), :] = _silu(z).astype(o_ref.dtype)
        return 0

    lax.fori_loop(0, ts // R, body, 0, unroll=2)


def _convmod(c, conv_w, conv_b, ln_g, ln_b, batch, seq):
    ts = CONV_TS
    spb = seq // ts
    wrep = jnp.broadcast_to(conv_w[:, None, :], (CONV_WIDTH, SUBLANES, CONV_CH))
    spec = pl.BlockSpec((ts, CONV_CH), lambda b, s: (b * spb + s, 0))
    vec = _resident((1, CONV_CH))
    return pl.pallas_call(
        _convmod_kernel,
        grid=(batch, spb),
        in_specs=[spec, _resident(wrep.shape), vec, vec, vec],
        out_specs=spec,
        out_shape=jax.ShapeDtypeStruct((batch * seq, CONV_CH), BF16),
        scratch_shapes=[pltpu.VMEM((CONV_HALO + ts, CONV_CH), F32),
                        pltpu.VMEM((SUBLANES, CONV_HALO + ts, CONV_CH), F32)],
        compiler_params=_params("parallel", "arbitrary"),
        name="convmod",
    )(c, wrep, conv_b.reshape(1, -1), ln_g.reshape(1, -1), ln_b.reshape(1, -1))


def _sb_kernel(q_ref, k_ref, v_ref, u_ref, o_ref, carry_ref, acc_ref):
    T = SB_TILE
    nblk = q_ref.shape[0] // T
    head0 = _head0_lanes((T, LANES))
    qpos = lax.broadcasted_iota(jnp.int32, (T, T), 0)
    kpos = lax.broadcasted_iota(jnp.int32, (T, T), 1)
    below_diag = kpos < qpos
    carry_ref[...] = jnp.zeros_like(carry_ref)
    acc_ref[...] = jnp.zeros_like(acc_ref)

    def score_stage(kb):
        rows = slice(kb * T, (kb + 1) * T)
        k_blk = k_ref[rows, :]
        chains = [(qb, h, qb == kb) for qb in range(kb, nblk) for h in range(2)]
        zs = []
        for qb, h, _ in chains:
            q = q_ref[qb * T:(qb + 1) * T, :]
            lanes_h = head0 if h == 0 else jnp.logical_not(head0)
            zs.append(_dot_nt(jnp.where(lanes_h, q, jnp.zeros_like(q)), k_blk))
        return chains, zs, v_ref[rows, :]

    def value_stage(chains, zs, v_blk):
        nls = []
        for (qb, h, masked), z in zip(chains, zs):
            neg_abs = lax.bitcast_convert_type(lax.bitcast_convert_type(z, jnp.uint32) | jnp.uint32(SIGN_BIT), F32)
            nl = jnp.maximum(z, 0.0) + jnp.log(1.0 + jnp.exp(neg_abs))
            if masked:
                nl = jnp.where(below_diag, nl, 0.0)
            nls.append(nl.astype(BF16))
        incls = [_dot(nl, u_ref[...]) for nl in nls]
        probs = []
        for (qb, h, masked), z, incl in zip(chains, zs, incls):
            carry = carry_ref[qb, h]
            a = jnp.exp(z - (incl + jnp.concatenate([carry] * (T // LANES), axis=1)))
            if masked:
                a = jnp.where(below_diag, a, 0.0)
            carry_ref[qb, h] = carry + jnp.broadcast_to(incl[:, 0:1], (T, LANES))
            probs.append(a.astype(BF16))
        for (qb, h, _), a in zip(chains, probs):
            acc_ref[qb, h] += _dot(a, v_blk)

    pending = score_stage(nblk - 1)
    for kb in reversed(range(nblk)):
        upcoming = score_stage(kb - 1) if kb > 0 else None
        value_stage(*pending)
        pending = upcoming
    for qb in range(nblk):
        out = jnp.where(head0, acc_ref[qb, 0], acc_ref[qb, 1])
        o_ref[qb * T:(qb + 1) * T, :] = out.astype(o_ref.dtype)


def _sb(dq, dk, dv, batch, seq):
    T = SB_TILE
    npairs = SB_HEADS // 2
    nblk = seq // T
    j = lax.broadcasted_iota(jnp.int32, (T, T), 0)
    s = lax.broadcasted_iota(jnp.int32, (T, T), 1)
    u = jnp.where(j >= s, 1.0, 0.0).astype(BF16)
    spec = pl.BlockSpec((seq, LANES), lambda b, p: (b, p))
    return pl.pallas_call(
        _sb_kernel,
        grid=(batch, npairs),
        in_specs=[spec, spec, spec, _resident((T, T))],
        out_specs=spec,
        out_shape=jax.ShapeDtypeStruct((batch * seq, SB_HEADS * HEAD_DIM), BF16),
        scratch_shapes=[pltpu.VMEM((nblk, 2, T, LANES), F32), pltpu.VMEM((nblk, 2, T, LANES), F32)],
        compiler_params=_params("parallel", "parallel"),
        name="stickbreak",
    )(dq, dk, dv, u)


def kernel(x, norm_mix0, w_in0, gla_wa2, gla_ba, gla_norm, w_out0, norm_ffn0, ffn_up0, ffn_conv0, ffn_down0,
           norm_mix1, w_in1, conv_w1, conv_b1, conv_ln_g1, conv_ln_b1, w_out1, norm_ffn1, ffn_up1, ffn_conv1,
           ffn_down1, final_norm):
    batch, seq, d = x.shape
    h = x.reshape(batch * seq, d)
    aq, ak, av, ag, la, bq, bk, bv = _inproj0(h, norm_mix0, w_in0, gla_wa2, gla_ba, seq)
    oa = _gla(aq, ak, la, av, ag, gla_norm, batch, seq)
    ob = _dsw(bq, bk, bv, batch, seq)
    h = _outproj_ffn(h, oa, ob, w_out0, norm_ffn0, ffn_up0, ffn_conv0, ffn_down0, final_norm, seq, False)
    c, dq, dk, dv = _inproj1(h, norm_mix1, w_in1)
    oc = _convmod(c, conv_w1, conv_b1, conv_ln_g1, conv_ln_b1, batch, seq)
    od = _sb(dq, dk, dv, batch, seq)
    h = _outproj_ffn(h, oc, od, w_out1, norm_ffn1, ffn_up1, ffn_conv1, ffn_down1, final_norm, seq, True)
    return h.reshape(batch, seq, d)
```

```python
import functools

import jax
import jax.numpy as jnp
from jax import lax
from jax.experimental import pallas as pl
from jax.experimental.pallas import tpu as pltpu

F32 = jnp.float32
BF16 = jnp.bfloat16

D_MODEL = 1024
HEAD_DIM = 64
EPS = 1e-6
GLA_HEADS = 4
GLA_DV = 128
GLA_DK = 64
GLA_RANK = 16
GLA_TAU = 16.0
GLA_CHUNK = 64
GLA_GROUP = 32
DSW_HEADS = 8
DSW_PATTERNS = ((128, 1), (512, 4), (2048, 16))
DSW_BLOCK = 128
CONV_CH = 512
CONV_WIDTH = 31
SB_HEADS = 8
D_FF = 2816
FFN_CONV = 3
ROPE_THETA = 500000.0
ROPE_DIMS = 16

LANES = 128
SUBLANES = 8
VMEM_LIMIT = 56 * 1024 * 1024
MASK_NEG = -1e30
SIGN_BIT = 0x80000000

PROJ_TM = 1024
FFN_TM = 512
FFN_TF = 256
CONV_TS = 1024
CONV_HALO = 32
CONV_RB = 32
SB_TILE = 256
DSW_GROUP = 2
DSW_COMBINE_ROWS = 256


def _dot(a, b):
    return jnp.dot(a, b, preferred_element_type=F32)


def _dot_nt(a, b):
    return lax.dot_general(a, b, (((1,), (1,)), ((), ())), preferred_element_type=F32)


def _dot_tn(a, b):
    return lax.dot_general(a, b, (((0,), (0,)), ((), ())), preferred_element_type=F32)


def _rmsnorm(x, g):
    return x * lax.rsqrt(jnp.mean(x * x, axis=-1, keepdims=True) + EPS) * g


def _softplus_neg_abs(z):
    return jnp.log1p(jnp.exp(-jnp.abs(z)))


def _log_sigmoid(z):
    return jnp.minimum(z, 0.0) - _softplus_neg_abs(z)


def _silu(x):
    return x * jax.nn.sigmoid(x)


def _head0_lanes(shape):
    return lax.broadcasted_iota(jnp.int32, shape, len(shape) - 1) < HEAD_DIM


def _resident(shape):
    nd = len(shape)
    return pl.BlockSpec(shape, lambda *_: (0,) * nd, pipeline_mode=pl.Buffered(1))


def _params(*sem):
    return pltpu.CompilerParams(dimension_semantics=sem, vmem_limit_bytes=VMEM_LIMIT)


def _rope(x, cos, sin_lo, sin_hi):
    half = ROPE_DIMS // 2
    return x * cos + pltpu.roll(x, LANES - half, 1) * sin_lo + pltpu.roll(x, half, 1) * sin_hi


def _inproj0_kernel(x_ref, g_ref, w_ref, wa2_ref, ba_ref, cos_ref, sinlo_ref, sinhi_ref,
                    aq_ref, ak_ref, av_ref, ag_ref, la_ref, bq_ref, bk_ref, bv_ref):
    xn = _rmsnorm(x_ref[...], g_ref[...]).astype(BF16)

    def proj(lo, hi):
        return _dot(xn, w_ref[:, lo:hi])

    aq_ref[...] = (proj(0, 256) * (GLA_DK ** -0.5)).astype(BF16)
    ak_ref[...] = proj(256, 512).astype(BF16)
    av_ref[...] = proj(512, 1024).astype(BF16)
    ag_ref[...] = proj(1024, 1536)
    ar = proj(3072, 3200)
    gate_logit = _dot(ar.astype(BF16), wa2_ref[...]) + ba_ref[...]
    la_ref[...] = _log_sigmoid(gate_logit) / GLA_TAU
    cos, sin_lo, sin_hi = cos_ref[...], sinlo_ref[...], sinhi_ref[...]
    bq = proj(1536, 2048)
    bk = proj(2048, 2560)
    for blk in range(4):
        sl = slice(blk * LANES, (blk + 1) * LANES)
        bq_ref[:, sl] = _rope(bq[:, sl], cos, sin_lo, sin_hi) * (HEAD_DIM ** -0.5)
        bk_ref[:, sl] = _rope(bk[:, sl], cos, sin_lo, sin_hi)
    bv_ref[...] = proj(2560, 3072)


def _rope_tables(seq):
    half = ROPE_DIMS // 2
    inv = ROPE_THETA ** (-jnp.arange(half, dtype=F32) / half)
    ang = jnp.arange(seq, dtype=F32)[:, None] * inv[None, :]
    cos, sin = jnp.cos(ang), jnp.sin(ang)
    ones = jnp.ones((seq, HEAD_DIM - ROPE_DIMS), F32)
    zeros = jnp.zeros((seq, HEAD_DIM - ROPE_DIMS), F32)
    zh = jnp.zeros((seq, half), F32)
    cos_h = jnp.concatenate([cos, cos, ones], axis=1)
    lo_h = jnp.concatenate([-sin, zh, zeros], axis=1)
    hi_h = jnp.concatenate([zh, sin, zeros], axis=1)
    two = lambda t: jnp.concatenate([t, t], axis=1)
    return two(cos_h), two(lo_h), two(hi_h)


def _inproj0(x2, g, w_in, wa2, ba, seq):
    T = x2.shape[0]
    tm = PROJ_TM
    aw_k, aw_v, bw = GLA_HEADS * GLA_DK, GLA_HEADS * GLA_DV, DSW_HEADS * HEAD_DIM
    o = [0, aw_k, 2 * aw_k, 2 * aw_k + aw_v, 2 * aw_k + 2 * aw_v]
    o.append(o[-1] + GLA_RANK)
    ar_cols = w_in[:, o[4]:o[5]]
    w = jnp.concatenate(
        [w_in[:, :o[4]], w_in[:, o[5]:], ar_cols, jnp.zeros((D_MODEL, LANES - GLA_RANK), F32)], axis=1
    ).astype(BF16)
    wa2p = jnp.concatenate([wa2, jnp.zeros((LANES - GLA_RANK, aw_k), F32)], axis=0).astype(BF16)
    cos, sin_lo, sin_hi = _rope_tables(seq)
    spt = seq // tm
    row = lambda n: pl.BlockSpec((tm, n), lambda i: (i, 0))
    tab = pl.BlockSpec((tm, LANES), lambda i: (i % spt, 0))
    outs = [(aw_k, BF16), (aw_k, BF16), (aw_v, BF16), (aw_v, F32), (aw_k, F32), (bw, F32), (bw, F32), (bw, F32)]
    return pl.pallas_call(
        _inproj0_kernel,
        grid=(T // tm,),
        in_specs=[row(D_MODEL), _resident((1, D_MODEL)), _resident(w.shape), _resident(wa2p.shape),
                  _resident((1, aw_k)), tab, tab, tab],
        out_specs=[row(n) for n, _ in outs],
        out_shape=[jax.ShapeDtypeStruct((T, n), dt) for n, dt in outs],
        compiler_params=_params("parallel"),
        name="inproj0",
    )(x2, g.reshape(1, -1), w, wa2p, ba.reshape(1, -1), cos, sin_lo, sin_hi)


def _gla_kernel(q_ref, k_ref, la_ref, v_ref, gate_ref, gn_ref, o_ref, st0_ref, st1_ref):
    C, G = GLA_CHUNK, GLA_GROUP
    seq = q_ref.shape[0]
    st_refs = (st0_ref, st1_ref)
    for st in st_refs:
        st[...] = jnp.zeros_like(st)
    head0 = _head0_lanes((C, LANES))
    r = lax.broadcasted_iota(jnp.int32, (C, C), 0)
    c = lax.broadcasted_iota(jnp.int32, (C, C), 1)
    tril = c <= r
    tri = jnp.where(tril, 1.0, 0.0).astype(BF16)
    tri3 = jnp.concatenate([tri, tri, tri], axis=1)
    gn = gn_ref[...]

    def body(it, _):
        rows = [pl.ds(pl.multiple_of((it * G + j) * C, C), C) for j in range(G)]
        la = jnp.concatenate([la_ref[rw, :] for rw in rows], axis=1)
        hi = la.astype(BF16)
        rem = la - hi.astype(F32)
        mid = rem.astype(BF16)
        lo = (rem - mid.astype(F32)).astype(BF16)
        bcum_all = _dot(tri3, jnp.concatenate([hi, mid, lo], axis=0))
        chains, decs = [], []
        for j in range(G):
            bcum = bcum_all[:, j * LANES:(j + 1) * LANES]
            btot = bcum[C - 1:C, :]
            q = q_ref[rows[j], :].astype(F32)
            k = k_ref[rows[j], :].astype(F32)
            q_dec = q * jnp.exp(bcum)
            k_inv = (k * jnp.exp(-bcum)).astype(BF16)
            k_tail = k * jnp.exp(btot - bcum)
            decs.append(jnp.exp(btot))
            for h in range(2):
                lanes_h = head0 if h == 0 else jnp.logical_not(head0)
                qh = jnp.where(lanes_h, q_dec, 0.0).astype(BF16)
                kth = jnp.where(lanes_h, k_tail, 0.0).astype(BF16)
                vh = v_ref[rows[j], h * GLA_DV:(h + 1) * GLA_DV]
                chains.append((j, h, qh, k_inv, kth, vh))
        scores = [jnp.where(tril, _dot_nt(qh, k_inv), 0.0).astype(BF16) for _, _, qh, k_inv, _, _ in chains]
        incs = [_dot_tn(vh, kth) for _, _, _, _, kth, vh in chains]
        states = {}
        for h in range(2):
            st = st_refs[h][...]
            for j in range(G):
                states[j, h] = st.astype(BF16)
                st = st * decs[j] + incs[2 * j + h]
            st_refs[h][...] = st
        outs = [_dot(sc, vh) + _dot_nt(qh, states[j, h])
                for sc, (j, h, qh, _, _, vh) in zip(scores, chains)]
        for o, (j, h, _, _, _, _) in zip(outs, chains):
            cols = slice(h * GLA_DV, (h + 1) * GLA_DV)
            y = _rmsnorm(o, gn) * _silu(gate_ref[rows[j], cols])
            o_ref[rows[j], cols] = y.astype(o_ref.dtype)
        return 0

    lax.fori_loop(0, seq // (C * G), body, 0)


def _gla(aq, ak, la, av, ag, gla_norm, batch, seq):
    npairs = GLA_HEADS // 2
    qk = pl.BlockSpec((seq, LANES), lambda b, p: (b, p))
    vv = pl.BlockSpec((seq, 2 * GLA_DV), lambda b, p: (b, p))
    return pl.pallas_call(
        _gla_kernel,
        grid=(batch, npairs),
        in_specs=[qk, qk, qk, vv, vv, _resident((1, GLA_DV))],
        out_specs=vv,
        out_shape=jax.ShapeDtypeStruct((batch * seq, GLA_HEADS * GLA_DV), BF16),
        scratch_shapes=[pltpu.VMEM((GLA_DV, LANES), F32), pltpu.VMEM((GLA_DV, LANES), F32)],
        compiler_params=_params("parallel", "parallel"),
        name="gla",
    )(aq, ak, la, av, ag, gla_norm.reshape(1, -1))


def _dsw_kernel(q_ref, k_ref, v_ref, o_ref, *stat_refs):
    C, G = DSW_BLOCK, DSW_GROUP
    seq = q_ref.shape[0]
    npat = len(DSW_PATTERNS)
    m_refs, l_refs, a_refs = stat_refs[0:npat], stat_refs[npat:2 * npat], stat_refs[2 * npat:3 * npat]
    head0 = _head0_lanes((C, LANES))
    qi = lax.broadcasted_iota(jnp.int32, (C, 2 * C), 0)
    ki = lax.broadcasted_iota(jnp.int32, (C, 2 * C), 1)
    valid_cur = jnp.logical_and(ki >= C, ki - C <= qi)
    valid_both = jnp.logical_or(jnp.logical_and(ki < C, ki >= qi), valid_cur)

    def rows(start, d):
        return pl.ds(start, C) if d == 1 else pl.ds(start, C, stride=d)

    def score_stage(pat, d, nb, it):
        base = it * G
        if nb > 1:
            assert nb % G == 0
            stream, blk0 = base // nb, base % nb
            starts = [stream + d * C * (blk0 + j) for j in range(G)]
            first = rows(max(starts[0] - d * C, 0), d)
            kb = [k_ref[first, :].astype(BF16)] + [k_ref[rows(s, d), :].astype(BF16) for s in starts]
            vb = [v_ref[first, :].astype(BF16)] + [v_ref[rows(s, d), :].astype(BF16) for s in starts]
            kcats = [jnp.concatenate([kb[j], kb[j + 1]], axis=0) for j in range(G)]
            vcats = [jnp.concatenate([vb[j], vb[j + 1]], axis=0) for j in range(G)]
            valids = [valid_both if blk0 + j > 0 else valid_cur for j in range(G)]
        else:
            starts = [base + j for j in range(G)]
            kcats = [k_ref[rows(s, d), :].astype(BF16) for s in starts]
            vcats = [v_ref[rows(s, d), :].astype(BF16) for s in starts]
            valids = [valid_cur[:, C:]] * G
        scores = []
        for j in range(G):
            q = q_ref[rows(starts[j], d), :]
            for h in range(2):
                lanes_h = head0 if h == 0 else jnp.logical_not(head0)
                scores.append(_dot_nt(jnp.where(lanes_h, q, 0.0).astype(BF16), kcats[j]))
        return pat, d, starts, scores, vcats, valids

    def value_stage(pat, d, starts, scores, vcats, valids):
        ones = jnp.ones((vcats[0].shape[0], LANES), BF16)
        vexts = [jnp.concatenate([v, ones], axis=1) for v in vcats]
        stats = []
        for i, s in enumerate(scores):
            s = jnp.where(valids[i // 2], s, MASK_NEG)
            m = jnp.max(s, axis=-1, keepdims=True)
            stats.append((m, jnp.exp(s - m).astype(BF16)))
        nums = [_dot(p, vexts[i // 2]) for i, (_, p) in enumerate(stats)]
        for j in range(G):
            cur = rows(starts[j], d)
            n0, n1 = nums[2 * j], nums[2 * j + 1]
            m_refs[pat][cur, :] = jnp.where(head0, stats[2 * j][0], stats[2 * j + 1][0])
            l_refs[pat][cur, :] = jnp.where(head0, n0[:, LANES:], n1[:, LANES:])
            a_refs[pat][cur, :] = jnp.where(head0, n0[:, :LANES], n1[:, :LANES])

    groups = []
    for pat, (window, d) in enumerate(DSW_PATTERNS):
        assert window // d == C
        groups += [(pat, d, seq // d // C, it) for it in range(seq // C // G)]
    pending = score_stage(*groups[0])
    for i in range(len(groups)):
        upcoming = score_stage(*groups[i + 1]) if i + 1 < len(groups) else None
        value_stage(*pending)
        pending = upcoming

    R = DSW_COMBINE_ROWS
    for r0 in range(0, seq, R):
        rows_r = slice(r0, r0 + R)
        ms = [m[rows_r, :] for m in m_refs]
        m_max = functools.reduce(jnp.maximum, ms)
        ws = [jnp.exp(m - m_max) for m in ms]
        num = sum(w * a[rows_r, :] for w, a in zip(ws, a_refs))
        den = sum(w * l[rows_r, :] for w, l in zip(ws, l_refs))
        o_ref[rows_r, :] = (num / den).astype(o_ref.dtype)


def _dsw(bq, bk, bv, batch, seq):
    npairs = DSW_HEADS // 2
    spec = pl.BlockSpec((seq, LANES), lambda b, p: (b, p))
    return pl.pallas_call(
        _dsw_kernel,
        grid=(batch, npairs),
        in_specs=[spec, spec, spec],
        out_specs=spec,
        out_shape=jax.ShapeDtypeStruct((batch * seq, DSW_HEADS * HEAD_DIM), BF16),
        scratch_shapes=[pltpu.VMEM((seq, LANES), F32) for _ in range(3 * len(DSW_PATTERNS))],
        compiler_params=_params("parallel", "parallel"),
        name="dsw",
    )(bq, bk, bv)


def _ffn_kernel(h_ref, ma_ref, mb_ref, wo_ref, gf_ref, wu_ref, wc_ref, wd_ref, gfin_ref,
                o_ref, carry_ref, ubuf_ref, xn_ref, act_ref, *, tiles_per_seq, final_norm):
    tm = h_ref.shape[0]
    half = ma_ref.shape[1]
    tf = FFN_TF
    nchunks = D_FF // tf
    H = SUBLANES

    @pl.when(pl.program_id(0) % tiles_per_seq == 0)
    def _():
        carry_ref[...] = jnp.zeros_like(carry_ref)

    h1 = h_ref[...] + _dot(ma_ref[...], wo_ref[0:half, :]) + _dot(mb_ref[...], wo_ref[half:, :])
    xn_ref[...] = _rmsnorm(h1, gf_ref[...]).astype(BF16)
    o_ref[...] = h1

    def up(c):
        for slot in range(2):
            buf = ubuf_ref.at[c % 2, slot]
            buf[0:H, :] = carry_ref[c, slot]
            buf[H:, :] = _dot(xn_ref[...], wu_ref[:, slot * D_FF + c * tf:slot * D_FF + (c + 1) * tf])
            carry_ref[c, slot] = buf[tm:tm + H, :]

    def conv(c, slot):
        buf = ubuf_ref.at[c % 2, slot]
        w = wc_ref[:, slot * D_FF + c * tf:slot * D_FF + (c + 1) * tf]
        out = w[FFN_CONV - 1:FFN_CONV, :] * buf[H:, :]
        for j in range(1, FFN_CONV):
            out = out + w[FFN_CONV - 1 - j:FFN_CONV - j, :] * buf[H - j:H - j + tm, :]
        return out

    up(0)
    for c in range(nchunks):
        if c + 1 < nchunks:
            up(c + 1)
        act_ref[:, c * tf:(c + 1) * tf] = (_silu(conv(c, 0)) * conv(c, 1)).astype(BF16)
    out = o_ref[...] + _dot(act_ref[...], wd_ref[...])
    if final_norm:
        out = _rmsnorm(out, gfin_ref[...])
    o_ref[...] = out


def _outproj_ffn(h2, mix_a, mix_b, w_out, g_ffn, w_up, w_conv, w_down, g_final, seq, final_norm):
    T = h2.shape[0]
    tm, tf = FFN_TM, FFN_TF
    nchunks = D_FF // tf
    half = mix_a.shape[1]
    wc = jnp.concatenate([w_conv, jnp.zeros((SUBLANES - FFN_CONV, 2 * D_FF), F32)], axis=0)
    wd = w_down.astype(BF16)
    row = lambda n: pl.BlockSpec((tm, n), lambda i: (i, 0))
    kern = functools.partial(_ffn_kernel, tiles_per_seq=seq // tm, final_norm=final_norm)
    return pl.pallas_call(
        kern,
        grid=(T // tm,),
        in_specs=[row(D_MODEL), row(half), row(half), _resident((2 * half, D_MODEL)), _resident((1, D_MODEL)),
                  _resident(w_up.shape), _resident(wc.shape), _resident(wd.shape), _resident((1, D_MODEL))],
        out_specs=row(D_MODEL),
        out_shape=jax.ShapeDtypeStruct((T, D_MODEL), F32),
        scratch_shapes=[pltpu.VMEM((nchunks, 2, SUBLANES, tf), F32),
                        pltpu.VMEM((2, 2, tm + SUBLANES, tf), F32),
                        pltpu.VMEM((tm, D_MODEL), BF16),
                        pltpu.VMEM((tm, D_FF), BF16)],
        compiler_params=_params("arbitrary"),
        name="outproj_ffn",
    )(h2, mix_a, mix_b, w_out.astype(BF16), g_ffn.reshape(1, -1), w_up.astype(BF16), wc, wd,
      g_final.reshape(1, -1))


def _inproj1_kernel(x_ref, g_ref, w_ref, c_ref, q_ref, k_ref, v_ref):
    xn = _rmsnorm(x_ref[...], g_ref[...]).astype(BF16)

    def proj(lo, hi):
        return _dot(xn, w_ref[:, lo:hi])

    c_ref[...] = proj(0, 512) * jax.nn.sigmoid(proj(512, 1024))
    q_ref[...] = (proj(1024, 1536) * (HEAD_DIM ** -0.5)).astype(BF16)
    k_ref[...] = proj(1536, 2048).astype(BF16)
    v_ref[...] = proj(2048, 2560).astype(BF16)


def _inproj1(x2, g, w_in):
    T = x2.shape[0]
    tm = PROJ_TM
    row = lambda n: pl.BlockSpec((tm, n), lambda i: (i, 0))
    outs = [(CONV_CH, F32), (512, BF16), (512, BF16), (512, BF16)]
    return pl.pallas_call(
        _inproj1_kernel,
        grid=(T // tm,),
        in_specs=[row(D_MODEL), _resident((1, D_MODEL)), _resident(w_in.shape)],
        out_specs=[row(n) for n, _ in outs],
        out_shape=[jax.ShapeDtypeStruct((T, n), dt) for n, dt in outs],
        compiler_params=_params("parallel"),
        name="inproj1",
    )(x2, g.reshape(1, -1), w_in.astype(BF16))


def _convmod_kernel(c_ref, w_ref, b_ref, lg_ref, lb_ref, o_ref, buf_ref, sh_ref):
    ts = c_ref.shape[0]
    HALO = CONV_HALO

    @pl.when(pl.program_id(1) == 0)
    def _():
        buf_ref[0:HALO, :] = jnp.zeros((HALO, CONV_CH), F32)

    @pl.when(pl.program_id(1) > 0)
    def _():
        buf_ref[0:HALO, :] = buf_ref[ts:ts + HALO, :]

    buf_ref[HALO:, :] = c_ref[...]
    n_sh = ts + HALO - SUBLANES
    for b in range(1, SUBLANES):
        sh_ref[b, 0:n_sh, :] = buf_ref[b:b + n_sh, :]
    bias, lg, lb = b_ref[...], lg_ref[...], lb_ref[...]
    R = CONV_RB

    for rb in range(ts // R):
        r0 = rb * R
        accs = [jnp.zeros((R // SUBLANES, SUBLANES, CONV_CH), F32) for _ in range(2)]
        for kk in range(CONV_WIDTH):
            off = HALO - (CONV_WIDTH - 1) + kk
            a, b = off // SUBLANES, off % SUBLANES
            rows = pl.ds(r0 + a * SUBLANES, R)
            x = buf_ref[rows, :] if b == 0 else sh_ref[b, rows, :]
            accs[kk % 2] = accs[kk % 2] + w_ref[kk][None] * x.reshape(R // SUBLANES, SUBLANES, CONV_CH)
        acc = (accs[0] + accs[1]).reshape(R, CONV_CH)
        y = acc + bias
        mu = jnp.mean(y, axis=-1, keepdims=True)
        yc = y - mu
        var = jnp.mean(yc * yc, axis=-1, keepdims=True)
        z = yc * lax.rsqrt(var + EPS) * lg + lb
        o_ref[pl.ds(r0, R), :] = _silu(z).astype(o_ref.dtype)


def _convmod(c, conv_w, conv_b, ln_g, ln_b, batch, seq):
    ts = CONV_TS
    spb = seq // ts
    wrep = jnp.broadcast_to(conv_w[:, None, :], (CONV_WIDTH, SUBLANES, CONV_CH))
    spec = pl.BlockSpec((ts, CONV_CH), lambda b, s: (b * spb + s, 0))
    vec = _resident((1, CONV_CH))
    return pl.pallas_call(
        _convmod_kernel,
        grid=(batch, spb),
        in_specs=[spec, _resident(wrep.shape), vec, vec, vec],
        out_specs=spec,
        out_shape=jax.ShapeDtypeStruct((batch * seq, CONV_CH), BF16),
        scratch_shapes=[pltpu.VMEM((CONV_HALO + ts, CONV_CH), F32),
                        pltpu.VMEM((SUBLANES, CONV_HALO + ts, CONV_CH), F32)],
        compiler_params=_params("parallel", "arbitrary"),
        name="convmod",
    )(c, wrep, conv_b.reshape(1, -1), ln_g.reshape(1, -1), ln_b.reshape(1, -1))


def _sb_kernel(q_ref, k_ref, v_ref, u_ref, o_ref, carry_ref, acc_ref):
    T = SB_TILE
    nblk = q_ref.shape[0] // T
    head0 = _head0_lanes((T, LANES))
    qpos = lax.broadcasted_iota(jnp.int32, (T, T), 0)
    kpos = lax.broadcasted_iota(jnp.int32, (T, T), 1)
    below_diag = kpos < qpos
    carry_ref[...] = jnp.zeros_like(carry_ref)
    acc_ref[...] = jnp.zeros_like(acc_ref)

    def score_stage(kb):
        rows = slice(kb * T, (kb + 1) * T)
        k_blk = k_ref[rows, :]
        chains = [(qb, h, qb == kb) for qb in range(kb, nblk) for h in range(2)]
        zs = []
        for qb, h, _ in chains:
            q = q_ref[qb * T:(qb + 1) * T, :]
            lanes_h = head0 if h == 0 else jnp.logical_not(head0)
            zs.append(_dot_nt(jnp.where(lanes_h, q, jnp.zeros_like(q)), k_blk))
        return chains, zs, v_ref[rows, :]

    def value_stage(chains, zs, v_blk):
        nls = []
        for (qb, h, masked), z in zip(chains, zs):
            neg_abs = lax.bitcast_convert_type(lax.bitcast_convert_type(z, jnp.uint32) | jnp.uint32(SIGN_BIT), F32)
            nl = jnp.maximum(z, 0.0) + jnp.log(1.0 + jnp.exp(neg_abs))
            if masked:
                nl = jnp.where(below_diag, nl, 0.0)
            nls.append(nl.astype(BF16))
        incls = [_dot(nl, u_ref[...]) for nl in nls]
        probs = []
        for (qb, h, masked), z, incl in zip(chains, zs, incls):
            carry = carry_ref[qb, h]
            a = jnp.exp(z - (incl + jnp.concatenate([carry] * (T // LANES), axis=1)))
            if masked:
                a = jnp.where(below_diag, a, 0.0)
            carry_ref[qb, h] = carry + jnp.broadcast_to(incl[:, 0:1], (T, LANES))
            probs.append(a.astype(BF16))
        for (qb, h, _), a in zip(chains, probs):
            acc_ref[qb, h] += _dot(a, v_blk)

    pending = score_stage(nblk - 1)
    for kb in reversed(range(nblk)):
        upcoming = score_stage(kb - 1) if kb > 0 else None
        value_stage(*pending)
        pending = upcoming
    for qb in range(nblk):
        out = jnp.where(head0, acc_ref[qb, 0], acc_ref[qb, 1])
        o_ref[qb * T:(qb + 1) * T, :] = out.astype(o_ref.dtype)


def _sb(dq, dk, dv, batch, seq):
    T = SB_TILE
    npairs = SB_HEADS // 2
    nblk = seq // T
    j = lax.broadcasted_iota(jnp.int32, (T, T), 0)
    s = lax.broadcasted_iota(jnp.int32, (T, T), 1)
    u = jnp.where(j >= s, 1.0, 0.0).astype(BF16)
    spec = pl.BlockSpec((seq, LANES), lambda b, p: (b, p))
    return pl.pallas_call(
        _sb_kernel,
        grid=(batch, npairs),
        in_specs=[spec, spec, spec, _resident((T, T))],
        out_specs=spec,
        out_shape=jax.ShapeDtypeStruct((batch * seq, SB_HEADS * HEAD_DIM), BF16),
        scratch_shapes=[pltpu.VMEM((nblk, 2, T, LANES), F32), pltpu.VMEM((nblk, 2, T, LANES), F32)],
        compiler_params=_params("parallel", "parallel"),
        name="stickbreak",
    )(dq, dk, dv, u)


def kernel(x, norm_mix0, w_in0, gla_wa2, gla_ba, gla_norm, w_out0, norm_ffn0, ffn_up0, ffn_conv0, ffn_down0,
           norm_mix1, w_in1, conv_w1, conv_b1, conv_ln_g1, conv_ln_b1, w_out1, norm_ffn1, ffn_up1, ffn_conv1,
           ffn_down1, final_norm):
    batch, seq, d = x.shape
    h = x.reshape(batch * seq, d)
    aq, ak, av, ag, la, bq, bk, bv = _inproj0(h, norm_mix0, w_in0, gla_wa2, gla_ba, seq)
    oa = _gla(aq, ak, la, av, ag, gla_norm, batch, seq)
    ob = _dsw(bq, bk, bv, batch, seq)
    h = _outproj_ffn(h, oa, ob, w_out0, norm_ffn0, ffn_up0, ffn_conv0, ffn_down0, final_norm, seq, False)
    c, dq, dk, dv = _inproj1(h, norm_mix1, w_in1)
    oc = _convmod(c, conv_w1, conv_b1, conv_ln_g1, conv_ln_b1, batch, seq)
    od = _sb(dq, dk, dv, batch, seq)
    h = _outproj_ffn(h, oc, od, w_out1, norm_ffn1, ffn_up1, ffn_conv1, ffn_down1, final_norm, seq, True)
    return h.reshape(batch, seq, d)
```

```python
import functools

import jax
import jax.numpy as jnp
from jax import lax
from jax.experimental import pallas as pl
from jax.experimental.pallas import tpu as pltpu

F32 = jnp.float32
BF16 = jnp.bfloat16

D_MODEL = 1024
HEAD_DIM = 64
EPS = 1e-6
GLA_HEADS = 4
GLA_DV = 128
GLA_DK = 64
GLA_RANK = 16
GLA_TAU = 16.0
GLA_CHUNK = 64
GLA_GROUP = 32
DSW_HEADS = 8
DSW_PATTERNS = ((128, 1), (512, 4), (2048, 16))
DSW_BLOCK = 128
CONV_CH = 512
CONV_WIDTH = 31
SB_HEADS = 8
D_FF = 2816
FFN_CONV = 3
ROPE_THETA = 500000.0
ROPE_DIMS = 16

LANES = 128
SUBLANES = 8
VMEM_LIMIT = 56 * 1024 * 1024
MASK_NEG = -1e30
SIGN_BIT = 0x80000000

PROJ_TM = 1024
FFN_TM = 512
FFN_TF = 256
CONV_TS = 1024
CONV_HALO = 32
CONV_RB = 32
SB_TILE = 256
DSW_GROUP = 2
DSW_COMBINE_ROWS = 256


def _dot(a, b):
    return jnp.dot(a, b, preferred_element_type=F32)


def _dot_nt(a, b):
    return lax.dot_general(a, b, (((1,), (1,)), ((), ())), preferred_element_type=F32)


def _dot_tn(a, b):
    return lax.dot_general(a, b, (((0,), (0,)), ((), ())), preferred_element_type=F32)


def _rmsnorm(x, g):
    return x * lax.rsqrt(jnp.mean(x * x, axis=-1, keepdims=True) + EPS) * g


def _softplus_neg_abs(z):
    return jnp.log1p(jnp.exp(-jnp.abs(z)))


def _log_sigmoid(z):
    return jnp.minimum(z, 0.0) - _softplus_neg_abs(z)


def _silu(x):
    return x * jax.nn.sigmoid(x)


def _head0_lanes(shape):
    return lax.broadcasted_iota(jnp.int32, shape, len(shape) - 1) < HEAD_DIM


def _resident(shape):
    nd = len(shape)
    return pl.BlockSpec(shape, lambda *_: (0,) * nd, pipeline_mode=pl.Buffered(1))


def _params(*sem):
    return pltpu.CompilerParams(dimension_semantics=sem, vmem_limit_bytes=VMEM_LIMIT)


def _rope(x, cos, sin_lo, sin_hi):
    half = ROPE_DIMS // 2
    return x * cos + pltpu.roll(x, LANES - half, 1) * sin_lo + pltpu.roll(x, half, 1) * sin_hi


def _inproj0_kernel(x_ref, g_ref, w_ref, wa2_ref, ba_ref, cos_ref, sinlo_ref, sinhi_ref,
                    aq_ref, ak_ref, av_ref, ag_ref, la_ref, bq_ref, bk_ref, bv_ref):
    xn = _rmsnorm(x_ref[...], g_ref[...]).astype(BF16)

    def proj(lo, hi):
        return _dot(xn, w_ref[:, lo:hi])

    aq_ref[...] = (proj(0, 256) * (GLA_DK ** -0.5)).astype(BF16)
    ak_ref[...] = proj(256, 512).astype(BF16)
    av_ref[...] = proj(512, 1024).astype(BF16)
    ag_ref[...] = proj(1024, 1536)
    ar = proj(3072, 3200)
    gate_logit = _dot(ar.astype(BF16), wa2_ref[...]) + ba_ref[...]
    la_ref[...] = _log_sigmoid(gate_logit) / GLA_TAU
    cos, sin_lo, sin_hi = cos_ref[...], sinlo_ref[...], sinhi_ref[...]
    bq = proj(1536, 2048)
    bk = proj(2048, 2560)
    for blk in range(4):
        sl = slice(blk * LANES, (blk + 1) * LANES)
        bq_ref[:, sl] = _rope(bq[:, sl], cos, sin_lo, sin_hi) * (HEAD_DIM ** -0.5)
        bk_ref[:, sl] = _rope(bk[:, sl], cos, sin_lo, sin_hi)
    bv_ref[...] = proj(2560, 3072)


def _rope_tables(seq):
    half = ROPE_DIMS // 2
    inv = ROPE_THETA ** (-jnp.arange(half, dtype=F32) / half)
    ang = jnp.arange(seq, dtype=F32)[:, None] * inv[None, :]
    cos, sin = jnp.cos(ang), jnp.sin(ang)
    ones = jnp.ones((seq, HEAD_DIM - ROPE_DIMS), F32)
    zeros = jnp.zeros((seq, HEAD_DIM - ROPE_DIMS), F32)
    zh = jnp.zeros((seq, half), F32)
    cos_h = jnp.concatenate([cos, cos, ones], axis=1)
    lo_h = jnp.concatenate([-sin, zh, zeros], axis=1)
    hi_h = jnp.concatenate([zh, sin, zeros], axis=1)
    two = lambda t: jnp.concatenate([t, t], axis=1)
    return two(cos_h), two(lo_h), two(hi_h)


def _inproj0(x2, g, w_in, wa2, ba, seq):
    T = x2.shape[0]
    tm = PROJ_TM
    aw_k, aw_v, bw = GLA_HEADS * GLA_DK, GLA_HEADS * GLA_DV, DSW_HEADS * HEAD_DIM
    o = [0, aw_k, 2 * aw_k, 2 * aw_k + aw_v, 2 * aw_k + 2 * aw_v]
    o.append(o[-1] + GLA_RANK)
    ar_cols = w_in[:, o[4]:o[5]]
    w = jnp.concatenate(
        [w_in[:, :o[4]], w_in[:, o[5]:], ar_cols, jnp.zeros((D_MODEL, LANES - GLA_RANK), F32)], axis=1
    ).astype(BF16)
    wa2p = jnp.concatenate([wa2, jnp.zeros((LANES - GLA_RANK, aw_k), F32)], axis=0).astype(BF16)
    cos, sin_lo, sin_hi = _rope_tables(seq)
    spt = seq // tm
    row = lambda n: pl.BlockSpec((tm, n), lambda i: (i, 0))
    tab = pl.BlockSpec((tm, LANES), lambda i: (i % spt, 0))
    outs = [(aw_k, BF16), (aw_k, BF16), (aw_v, BF16), (aw_v, F32), (aw_k, F32), (bw, F32), (bw, F32), (bw, F32)]
    return pl.pallas_call(
        _inproj0_kernel,
        grid=(T // tm,),
        in_specs=[row(D_MODEL), _resident((1, D_MODEL)), _resident(w.shape), _resident(wa2p.shape),
                  _resident((1, aw_k)), tab, tab, tab],
        out_specs=[row(n) for n, _ in outs],
        out_shape=[jax.ShapeDtypeStruct((T, n), dt) for n, dt in outs],
        compiler_params=_params("parallel"),
        name="inproj0",
    )(x2, g.reshape(1, -1), w, wa2p, ba.reshape(1, -1), cos, sin_lo, sin_hi)


def _gla_kernel(q_ref, k_ref, la_ref, v_ref, gate_ref, gn_ref, o_ref, st0_ref, st1_ref):
    C, G = GLA_CHUNK, GLA_GROUP
    seq = q_ref.shape[0]
    st_refs = (st0_ref, st1_ref)
    for st in st_refs:
        st[...] = jnp.zeros_like(st)
    head0 = _head0_lanes((C, LANES))
    r = lax.broadcasted_iota(jnp.int32, (C, C), 0)
    c = lax.broadcasted_iota(jnp.int32, (C, C), 1)
    tril = c <= r
    tri = jnp.where(tril, 1.0, 0.0).astype(BF16)
    tri3 = jnp.concatenate([tri, tri, tri], axis=1)
    gn = gn_ref[...]

    def body(it, _):
        rows = [pl.ds(pl.multiple_of((it * G + j) * C, C), C) for j in range(G)]
        la = jnp.concatenate([la_ref[rw, :] for rw in rows], axis=1)
        hi = la.astype(BF16)
        rem = la - hi.astype(F32)
        mid = rem.astype(BF16)
        lo = (rem - mid.astype(F32)).astype(BF16)
        bcum_all = _dot(tri3, jnp.concatenate([hi, mid, lo], axis=0))
        chains, decs = [], []
        for j in range(G):
            bcum = bcum_all[:, j * LANES:(j + 1) * LANES]
            btot = bcum[C - 1:C, :]
            q = q_ref[rows[j], :].astype(F32)
            k = k_ref[rows[j], :].astype(F32)
            q_dec = q * jnp.exp(bcum)
            k_inv = (k * jnp.exp(-bcum)).astype(BF16)
            k_tail = k * jnp.exp(btot - bcum)
            decs.append(jnp.exp(btot))
            for h in range(2):
                lanes_h = head0 if h == 0 else jnp.logical_not(head0)
                qh = jnp.where(lanes_h, q_dec, 0.0).astype(BF16)
                kth = jnp.where(lanes_h, k_tail, 0.0).astype(BF16)
                vh = v_ref[rows[j], h * GLA_DV:(h + 1) * GLA_DV]
                chains.append((j, h, qh, k_inv, kth, vh))
        scores = [jnp.where(tril, _dot_nt(qh, k_inv), 0.0).astype(BF16) for _, _, qh, k_inv, _, _ in chains]
        incs = [_dot_tn(vh, kth) for _, _, _, _, kth, vh in chains]
        states = {}
        for h in range(2):
            st = st_refs[h][...]
            for j in range(G):
                states[j, h] = st.astype(BF16)
                st = st * decs[j] + incs[2 * j + h]
            st_refs[h][...] = st
        outs = [_dot(sc, vh) + _dot_nt(qh, states[j, h])
                for sc, (j, h, qh, _, _, vh) in zip(scores, chains)]
        for o, (j, h, _, _, _, _) in zip(outs, chains):
            cols = slice(h * GLA_DV, (h + 1) * GLA_DV)
            y = _rmsnorm(o, gn) * _silu(gate_ref[rows[j], cols])
            o_ref[rows[j], cols] = y.astype(o_ref.dtype)
        return 0

    lax.fori_loop(0, seq // (C * G), body, 0)


def _gla(aq, ak, la, av, ag, gla_norm, batch, seq):
    npairs = GLA_HEADS // 2
    qk = pl.BlockSpec((seq, LANES), lambda b, p: (b, p))
    vv = pl.BlockSpec((seq, 2 * GLA_DV), lambda b, p: (b, p))
    return pl.pallas_call(
        _gla_kernel,
        grid=(batch, npairs),
        in_specs=[qk, qk, qk, vv, vv, _resident((1, GLA_DV))],
        out_specs=vv,
        out_shape=jax.ShapeDtypeStruct((batch * seq, GLA_HEADS * GLA_DV), BF16),
        scratch_shapes=[pltpu.VMEM((GLA_DV, LANES), F32), pltpu.VMEM((GLA_DV, LANES), F32)],
        compiler_params=_params("parallel", "parallel"),
        name="gla",
    )(aq, ak, la, av, ag, gla_norm.reshape(1, -1))


def _dsw_kernel(q_ref, k_ref, v_ref, o_ref, *stat_refs):
    C, G = DSW_BLOCK, DSW_GROUP
    seq = q_ref.shape[0]
    npat = len(DSW_PATTERNS)
    m_refs, l_refs, a_refs = stat_refs[0:npat], stat_refs[npat:2 * npat], stat_refs[2 * npat:3 * npat]
    head0 = _head0_lanes((C, LANES))
    qi = lax.broadcasted_iota(jnp.int32, (C, 2 * C), 0)
    ki = lax.broadcasted_iota(jnp.int32, (C, 2 * C), 1)
    valid_cur = jnp.logical_and(ki >= C, ki - C <= qi)
    valid_both = jnp.logical_or(jnp.logical_and(ki < C, ki >= qi), valid_cur)

    def rows(start, d):
        return pl.ds(start, C) if d == 1 else pl.ds(start, C, stride=d)

    def score_stage(pat, d, nb, it):
        base = it * G
        if nb > 1:
            assert nb % G == 0
            stream, blk0 = base // nb, base % nb
            starts = [stream + d * C * (blk0 + j) for j in range(G)]
            first = rows(max(starts[0] - d * C, 0), d)
            kb = [k_ref[first, :].astype(BF16)] + [k_ref[rows(s, d), :].astype(BF16) for s in starts]
            vb = [v_ref[first, :].astype(BF16)] + [v_ref[rows(s, d), :].astype(BF16) for s in starts]
            kcats = [jnp.concatenate([kb[j], kb[j + 1]], axis=0) for j in range(G)]
            vcats = [jnp.concatenate([vb[j], vb[j + 1]], axis=0) for j in range(G)]
            valids = [valid_both if blk0 + j > 0 else valid_cur for j in range(G)]
        else:
            starts = [base + j for j in range(G)]
            kcats = [k_ref[rows(s, d), :].astype(BF16) for s in starts]
            vcats = [v_ref[rows(s, d), :].astype(BF16) for s in starts]
            valids = [valid_cur[:, C:]] * G
        scores = []
        for j in range(G):
            q = q_ref[rows(starts[j], d), :]
            for h in range(2):
                lanes_h = head0 if h == 0 else jnp.logical_not(head0)
                scores.append(_dot_nt(jnp.where(lanes_h, q, 0.0).astype(BF16), kcats[j]))
        return pat, d, starts, scores, vcats, valids

    def value_stage(pat, d, starts, scores, vcats, valids):
        ones = jnp.ones((vcats[0].shape[0], LANES), BF16)
        vexts = [jnp.concatenate([v, ones], axis=1) for v in vcats]
        stats = []
        for i, s in enumerate(scores):
            s = jnp.where(valids[i // 2], s, MASK_NEG)
            m = jnp.max(s, axis=-1, keepdims=True)
            stats.append((m, jnp.exp((s - m).astype(BF16))))
        nums = [_dot(p, vexts[i // 2]) for i, (_, p) in enumerate(stats)]
        for j in range(G):
            cur = rows(starts[j], d)
            n0, n1 = nums[2 * j], nums[2 * j + 1]
            m_refs[pat][cur, :] = jnp.where(head0, stats[2 * j][0], stats[2 * j + 1][0])
            l_refs[pat][cur, :] = jnp.where(head0, n0[:, LANES:], n1[:, LANES:])
            a_refs[pat][cur, :] = jnp.where(head0, n0[:, :LANES], n1[:, :LANES])

    groups = []
    for pat, (window, d) in enumerate(DSW_PATTERNS):
        assert window // d == C
        groups += [(pat, d, seq // d // C, it) for it in range(seq // C // G)]
    pending = score_stage(*groups[0])
    for i in range(len(groups)):
        upcoming = score_stage(*groups[i + 1]) if i + 1 < len(groups) else None
        value_stage(*pending)
        pending = upcoming

    R = DSW_COMBINE_ROWS
    for r0 in range(0, seq, R):
        rows_r = slice(r0, r0 + R)
        ms = [m[rows_r, :] for m in m_refs]
        m_max = functools.reduce(jnp.maximum, ms)
        ws = [jnp.exp(m - m_max) for m in ms]
        num = sum(w * a[rows_r, :] for w, a in zip(ws, a_refs))
        den = sum(w * l[rows_r, :] for w, l in zip(ws, l_refs))
        o_ref[rows_r, :] = (num / den).astype(o_ref.dtype)


def _dsw(bq, bk, bv, batch, seq):
    npairs = DSW_HEADS // 2
    spec = pl.BlockSpec((seq, LANES), lambda b, p: (b, p))
    return pl.pallas_call(
        _dsw_kernel,
        grid=(batch, npairs),
        in_specs=[spec, spec, spec],
        out_specs=spec,
        out_shape=jax.ShapeDtypeStruct((batch * seq, DSW_HEADS * HEAD_DIM), BF16),
        scratch_shapes=[pltpu.VMEM((seq, LANES), F32) for _ in range(3 * len(DSW_PATTERNS))],
        compiler_params=_params("parallel", "parallel"),
        name="dsw",
    )(bq, bk, bv)


def _ffn_kernel(h_ref, ma_ref, mb_ref, wo_ref, gf_ref, wu_ref, wc_ref, wd_ref, gfin_ref,
                o_ref, carry_ref, ubuf_ref, xn_ref, act_ref, *, tiles_per_seq, final_norm):
    tm = h_ref.shape[0]
    half = ma_ref.shape[1]
    tf = FFN_TF
    nchunks = D_FF // tf
    H = SUBLANES

    @pl.when(pl.program_id(0) % tiles_per_seq == 0)
    def _():
        carry_ref[...] = jnp.zeros_like(carry_ref)

    h1 = h_ref[...] + _dot(ma_ref[...], wo_ref[0:half, :]) + _dot(mb_ref[...], wo_ref[half:, :])
    xn_ref[...] = _rmsnorm(h1, gf_ref[...]).astype(BF16)
    o_ref[...] = h1

    def up(c):
        for slot in range(2):
            buf = ubuf_ref.at[c % 2, slot]
            buf[0:H, :] = carry_ref[c, slot]
            buf[H:, :] = _dot(xn_ref[...], wu_ref[:, slot * D_FF + c * tf:slot * D_FF + (c + 1) * tf])
            carry_ref[c, slot] = buf[tm:tm + H, :]

    def conv(c, slot):
        buf = ubuf_ref.at[c % 2, slot]
        w = wc_ref[:, slot * D_FF + c * tf:slot * D_FF + (c + 1) * tf]
        out = w[FFN_CONV - 1:FFN_CONV, :] * buf[H:, :]
        for j in range(1, FFN_CONV):
            out = out + w[FFN_CONV - 1 - j:FFN_CONV - j, :] * buf[H - j:H - j + tm, :]
        return out

    up(0)
    for c in range(nchunks):
        if c + 1 < nchunks:
            up(c + 1)
        act_ref[:, c * tf:(c + 1) * tf] = (_silu(conv(c, 0)) * conv(c, 1)).astype(BF16)
    out = o_ref[...] + _dot(act_ref[...], wd_ref[...])
    if final_norm:
        out = _rmsnorm(out, gfin_ref[...])
    o_ref[...] = out


def _outproj_ffn(h2, mix_a, mix_b, w_out, g_ffn, w_up, w_conv, w_down, g_final, seq, final_norm):
    T = h2.shape[0]
    tm, tf = FFN_TM, FFN_TF
    nchunks = D_FF // tf
    half = mix_a.shape[1]
    wc = jnp.concatenate([w_conv, jnp.zeros((SUBLANES - FFN_CONV, 2 * D_FF), F32)], axis=0)
    wd = w_down.astype(BF16)
    row = lambda n: pl.BlockSpec((tm, n), lambda i: (i, 0))
    kern = functools.partial(_ffn_kernel, tiles_per_seq=seq // tm, final_norm=final_norm)
    return pl.pallas_call(
        kern,
        grid=(T // tm,),
        in_specs=[row(D_MODEL), row(half), row(half), _resident((2 * half, D_MODEL)), _resident((1, D_MODEL)),
                  _resident(w_up.shape), _resident(wc.shape), _resident(wd.shape), _resident((1, D_MODEL))],
        out_specs=row(D_MODEL),
        out_shape=jax.ShapeDtypeStruct((T, D_MODEL), F32),
        scratch_shapes=[pltpu.VMEM((nchunks, 2, SUBLANES, tf), F32),
                        pltpu.VMEM((2, 2, tm + SUBLANES, tf), F32),
                        pltpu.VMEM((tm, D_MODEL), BF16),
                        pltpu.VMEM((tm, D_FF), BF16)],
        compiler_params=_params("arbitrary"),
        name="outproj_ffn",
    )(h2, mix_a, mix_b, w_out.astype(BF16), g_ffn.reshape(1, -1), w_up.astype(BF16), wc, wd,
      g_final.reshape(1, -1))


def _inproj1_kernel(x_ref, g_ref, w_ref, c_ref, q_ref, k_ref, v_ref):
    xn = _rmsnorm(x_ref[...], g_ref[...]).astype(BF16)

    def proj(lo, hi):
        return _dot(xn, w_ref[:, lo:hi])

    c_ref[...] = proj(0, 512) * jax.nn.sigmoid(proj(512, 1024))
    q_ref[...] = (proj(1024, 1536) * (HEAD_DIM ** -0.5)).astype(BF16)
    k_ref[...] = proj(1536, 2048).astype(BF16)
    v_ref[...] = proj(2048, 2560).astype(BF16)


def _inproj1(x2, g, w_in):
    T = x2.shape[0]
    tm = PROJ_TM
    row = lambda n: pl.BlockSpec((tm, n), lambda i: (i, 0))
    outs = [(CONV_CH, F32), (512, BF16), (512, BF16), (512, BF16)]
    return pl.pallas_call(
        _inproj1_kernel,
        grid=(T // tm,),
        in_specs=[row(D_MODEL), _resident((1, D_MODEL)), _resident(w_in.shape)],
        out_specs=[row(n) for n, _ in outs],
        out_shape=[jax.ShapeDtypeStruct((T, n), dt) for n, dt in outs],
        compiler_params=_params("parallel"),
        name="inproj1",
    )(x2, g.reshape(1, -1), w_in.astype(BF16))


def _convmod_kernel(c_ref, w_ref, b_ref, lg_ref, lb_ref, o_ref, buf_ref, sh_ref):
    ts = c_ref.shape[0]
    HALO = CONV_HALO

    @pl.when(pl.program_id(1) == 0)
    def _():
        buf_ref[0:HALO, :] = jnp.zeros((HALO, CONV_CH), F32)

    @pl.when(pl.program_id(1) > 0)
    def _():
        buf_ref[0:HALO, :] = buf_ref[ts:ts + HALO, :]

    buf_ref[HALO:, :] = c_ref[...]
    n_sh = ts + HALO - SUBLANES
    for b in range(1, SUBLANES):
        sh_ref[b, 0:n_sh, :] = buf_ref[b:b + n_sh, :]
    bias, lg, lb = b_ref[...], lg_ref[...], lb_ref[...]
    R = CONV_RB

    for rb in range(ts // R):
        r0 = rb * R
        accs = [jnp.zeros((R // SUBLANES, SUBLANES, CONV_CH), F32) for _ in range(2)]
        for kk in range(CONV_WIDTH):
            off = HALO - (CONV_WIDTH - 1) + kk
            a, b = off // SUBLANES, off % SUBLANES
            rows = pl.ds(r0 + a * SUBLANES, R)
            x = buf_ref[rows, :] if b == 0 else sh_ref[b, rows, :]
            accs[kk % 2] = accs[kk % 2] + w_ref[kk][None] * x.reshape(R // SUBLANES, SUBLANES, CONV_CH)
        acc = (accs[0] + accs[1]).reshape(R, CONV_CH)
        y = acc + bias
        mu = jnp.mean(y, axis=-1, keepdims=True)
        yc = y - mu
        var = jnp.mean(yc * yc, axis=-1, keepdims=True)
        z = yc * lax.rsqrt(var + EPS) * lg + lb
        o_ref[pl.ds(r0, R), :] = _silu(z).astype(o_ref.dtype)


def _convmod(c, conv_w, conv_b, ln_g, ln_b, batch, seq):
    ts = CONV_TS
    spb = seq // ts
    wrep = jnp.broadcast_to(conv_w[:, None, :], (CONV_WIDTH, SUBLANES, CONV_CH))
    spec = pl.BlockSpec((ts, CONV_CH), lambda b, s: (b * spb + s, 0))
    vec = _resident((1, CONV_CH))
    return pl.pallas_call(
        _convmod_kernel,
        grid=(batch, spb),
        in_specs=[spec, _resident(wrep.shape), vec, vec, vec],
        out_specs=spec,
        out_shape=jax.ShapeDtypeStruct((batch * seq, CONV_CH), BF16),
        scratch_shapes=[pltpu.VMEM((CONV_HALO + ts, CONV_CH), F32),
                        pltpu.VMEM((SUBLANES, CONV_HALO + ts, CONV_CH), F32)],
        compiler_params=_params("parallel", "arbitrary"),
        name="convmod",
    )(c, wrep, conv_b.reshape(1, -1), ln_g.reshape(1, -1), ln_b.reshape(1, -1))


def _sb_kernel(q_ref, k_ref, v_ref, u_ref, o_ref, carry_ref, acc_ref):
    T = SB_TILE
    nblk = q_ref.shape[0] // T
    head0 = _head0_lanes((T, LANES))
    qpos = lax.broadcasted_iota(jnp.int32, (T, T), 0)
    kpos = lax.broadcasted_iota(jnp.int32, (T, T), 1)
    below_diag = kpos < qpos
    carry_ref[...] = jnp.zeros_like(carry_ref)
    acc_ref[...] = jnp.zeros_like(acc_ref)

    def score_stage(kb):
        rows = slice(kb * T, (kb + 1) * T)
        k_blk = k_ref[rows, :]
        chains = [(qb, h, qb == kb) for qb in range(kb, nblk) for h in range(2)]
        zs = []
        for qb, h, _ in chains:
            q = q_ref[qb * T:(qb + 1) * T, :]
            lanes_h = head0 if h == 0 else jnp.logical_not(head0)
            zs.append(_dot_nt(jnp.where(lanes_h, q, jnp.zeros_like(q)), k_blk))
        return chains, zs, v_ref[rows, :]

    def value_stage(chains, zs, v_blk):
        nls = []
        for (qb, h, masked), z in zip(chains, zs):
            neg_abs = lax.bitcast_convert_type(lax.bitcast_convert_type(z, jnp.uint32) | jnp.uint32(SIGN_BIT), F32)
            nl = jnp.maximum(z, 0.0) + jnp.log(1.0 + jnp.exp(neg_abs))
            if masked:
                nl = jnp.where(below_diag, nl, 0.0)
            nls.append(nl.astype(BF16))
        incls = [_dot(nl, u_ref[...]) for nl in nls]
        probs = []
        for (qb, h, masked), z, incl in zip(chains, zs, incls):
            carry = carry_ref[qb, h]
            a = jnp.exp((z - (incl + jnp.concatenate([carry] * (T // LANES), axis=1))).astype(BF16))
            if masked:
                a = jnp.where(below_diag, a, jnp.zeros_like(a))
            carry_ref[qb, h] = carry + jnp.broadcast_to(incl[:, 0:1], (T, LANES))
            probs.append(a)
        for (qb, h, _), a in zip(chains, probs):
            acc_ref[qb, h] += _dot(a, v_blk)

    pending = score_stage(nblk - 1)
    for kb in reversed(range(nblk)):
        upcoming = score_stage(kb - 1) if kb > 0 else None
        value_stage(*pending)
        pending = upcoming
    for qb in range(nblk):
        out = jnp.where(head0, acc_ref[qb, 0], acc_ref[qb, 1])
        o_ref[qb * T:(qb + 1) * T, :] = out.astype(o_ref.dtype)


def _sb(dq, dk, dv, batch, seq):
    T = SB_TILE
    npairs = SB_HEADS // 2
    nblk = seq // T
    j = lax.broadcasted_iota(jnp.int32, (T, T), 0)
    s = lax.broadcasted_iota(jnp.int32, (T, T), 1)
    u = jnp.where(j >= s, 1.0, 0.0).astype(BF16)
    spec = pl.BlockSpec((seq, LANES), lambda b, p: (b, p))
    return pl.pallas_call(
        _sb_kernel,
        grid=(batch, npairs),
        in_specs=[spec, spec, spec, _resident((T, T))],
        out_specs=spec,
        out_shape=jax.ShapeDtypeStruct((batch * seq, SB_HEADS * HEAD_DIM), BF16),
        scratch_shapes=[pltpu.VMEM((nblk, 2, T, LANES), F32), pltpu.VMEM((nblk, 2, T, LANES), F32)],
        compiler_params=_params("parallel", "parallel"),
        name="stickbreak",
    )(dq, dk, dv, u)


def kernel(x, norm_mix0, w_in0, gla_wa2, gla_ba, gla_norm, w_out0, norm_ffn0, ffn_up0, ffn_conv0, ffn_down0,
           norm_mix1, w_in1, conv_w1, conv_b1, conv_ln_g1, conv_ln_b1, w_out1, norm_ffn1, ffn_up1, ffn_conv1,
           ffn_down1, final_norm):
    batch, seq, d = x.shape
    h = x.reshape(batch * seq, d)
    aq, ak, av, ag, la, bq, bk, bv = _inproj0(h, norm_mix0, w_in0, gla_wa2, gla_ba, seq)
    oa = _gla(aq, ak, la, av, ag, gla_norm, batch, seq)
    ob = _dsw(bq, bk, bv, batch, seq)
    h = _outproj_ffn(h, oa, ob, w_out0, norm_ffn0, ffn_up0, ffn_conv0, ffn_down0, final_norm, seq, False)
    c, dq, dk, dv = _inproj1(h, norm_mix1, w_in1)
    oc = _convmod(c, conv_w1, conv_b1, conv_ln_g1, conv_ln_b1, batch, seq)
    od = _sb(dq, dk, dv, batch, seq)
    h = _outproj_ffn(h, oc, od, w_out1, norm_ffn1, ffn_up1, ffn_conv1, ffn_down1, final_norm, seq, True)
    return h.reshape(batch, seq, d)
```

```python
import functools

import jax
import jax.numpy as jnp
from jax import lax
from jax.experimental import pallas as pl
from jax.experimental.pallas import tpu as pltpu

F32 = jnp.float32
BF16 = jnp.bfloat16

D_MODEL = 1024
HEAD_DIM = 64
EPS = 1e-6
GLA_HEADS = 4
GLA_DV = 128
GLA_DK = 64
GLA_RANK = 16
GLA_TAU = 16.0
GLA_CHUNK = 64
GLA_GROUP = 32
DSW_HEADS = 8
DSW_PATTERNS = ((128, 1), (512, 4), (2048, 16))
DSW_BLOCK = 128
CONV_CH = 512
CONV_WIDTH = 31
SB_HEADS = 8
D_FF = 2816
FFN_CONV = 3
ROPE_THETA = 500000.0
ROPE_DIMS = 16

LANES = 128
SUBLANES = 8
VMEM_LIMIT = 56 * 1024 * 1024
MASK_NEG = -1e30
SIGN_BIT = 0x80000000

PROJ_TM = 1024
FFN_TM = 512
FFN_TF = 256
CONV_HALO = 32
CONV_RB = 32
SB_TILE = 256
DSW_GROUP = 2
DSW_COMBINE_ROWS = 256


def _dot(a, b):
    return jnp.dot(a, b, preferred_element_type=F32)


def _dot_nt(a, b):
    return lax.dot_general(a, b, (((1,), (1,)), ((), ())), preferred_element_type=F32)


def _dot_tn(a, b):
    return lax.dot_general(a, b, (((0,), (0,)), ((), ())), preferred_element_type=F32)


def _rmsnorm(x, g):
    return x * lax.rsqrt(jnp.mean(x * x, axis=-1, keepdims=True) + EPS) * g


def _softplus_neg_abs(z):
    return jnp.log1p(jnp.exp(-jnp.abs(z)))


def _log_sigmoid(z):
    return jnp.minimum(z, 0.0) - _softplus_neg_abs(z)


def _silu(x):
    return x * jax.nn.sigmoid(x)


def _head0_lanes(shape):
    return lax.broadcasted_iota(jnp.int32, shape, len(shape) - 1) < HEAD_DIM


def _resident(shape):
    nd = len(shape)
    return pl.BlockSpec(shape, lambda *_: (0,) * nd, pipeline_mode=pl.Buffered(1))


def _params(*sem):
    return pltpu.CompilerParams(dimension_semantics=sem, vmem_limit_bytes=VMEM_LIMIT)


def _rope(x, cos, sin_lo, sin_hi):
    half = ROPE_DIMS // 2
    return x * cos + pltpu.roll(x, LANES - half, 1) * sin_lo + pltpu.roll(x, half, 1) * sin_hi


def _inproj0_kernel(x_ref, g_ref, w_ref, wa2_ref, ba_ref, cos_ref, sinlo_ref, sinhi_ref,
                    aq_ref, ak_ref, av_ref, ag_ref, la_ref, bq_ref, bk_ref, bv_ref):
    xn = _rmsnorm(x_ref[...], g_ref[...]).astype(BF16)

    def proj(lo, hi):
        return _dot(xn, w_ref[:, lo:hi])

    aq_ref[...] = (proj(0, 256) * (GLA_DK ** -0.5)).astype(BF16)
    ak_ref[...] = proj(256, 512).astype(BF16)
    av_ref[...] = proj(512, 1024).astype(BF16)
    ag_ref[...] = proj(1024, 1536)
    ar = proj(3072, 3200)
    gate_logit = _dot(ar.astype(BF16), wa2_ref[...]) + ba_ref[...]
    la_ref[...] = _log_sigmoid(gate_logit) / GLA_TAU
    cos, sin_lo, sin_hi = cos_ref[...], sinlo_ref[...], sinhi_ref[...]
    bq = proj(1536, 2048)
    bk = proj(2048, 2560)
    for blk in range(4):
        sl = slice(blk * LANES, (blk + 1) * LANES)
        bq_ref[:, sl] = _rope(bq[:, sl], cos, sin_lo, sin_hi) * (HEAD_DIM ** -0.5)
        bk_ref[:, sl] = _rope(bk[:, sl], cos, sin_lo, sin_hi)
    bv_ref[...] = proj(2560, 3072)


def _rope_tables(seq):
    half = ROPE_DIMS // 2
    inv = ROPE_THETA ** (-jnp.arange(half, dtype=F32) / half)
    ang = jnp.arange(seq, dtype=F32)[:, None] * inv[None, :]
    cos, sin = jnp.cos(ang), jnp.sin(ang)
    ones = jnp.ones((seq, HEAD_DIM - ROPE_DIMS), F32)
    zeros = jnp.zeros((seq, HEAD_DIM - ROPE_DIMS), F32)
    zh = jnp.zeros((seq, half), F32)
    cos_h = jnp.concatenate([cos, cos, ones], axis=1)
    lo_h = jnp.concatenate([-sin, zh, zeros], axis=1)
    hi_h = jnp.concatenate([zh, sin, zeros], axis=1)
    two = lambda t: jnp.concatenate([t, t], axis=1)
    return two(cos_h), two(lo_h), two(hi_h)


def _inproj0(x2, g, w_in, wa2, ba, seq):
    T = x2.shape[0]
    tm = PROJ_TM
    aw_k, aw_v, bw = GLA_HEADS * GLA_DK, GLA_HEADS * GLA_DV, DSW_HEADS * HEAD_DIM
    o = [0, aw_k, 2 * aw_k, 2 * aw_k + aw_v, 2 * aw_k + 2 * aw_v]
    o.append(o[-1] + GLA_RANK)
    ar_cols = w_in[:, o[4]:o[5]]
    w = jnp.concatenate(
        [w_in[:, :o[4]], w_in[:, o[5]:], ar_cols, jnp.zeros((D_MODEL, LANES - GLA_RANK), F32)], axis=1
    ).astype(BF16)
    wa2p = jnp.concatenate([wa2, jnp.zeros((LANES - GLA_RANK, aw_k), F32)], axis=0).astype(BF16)
    cos, sin_lo, sin_hi = _rope_tables(seq)
    spt = seq // tm
    row = lambda n: pl.BlockSpec((tm, n), lambda i: (i, 0))
    tab = pl.BlockSpec((tm, LANES), lambda i: (i % spt, 0))
    outs = [(aw_k, BF16), (aw_k, BF16), (aw_v, BF16), (aw_v, F32), (aw_k, F32), (bw, F32), (bw, F32), (bw, F32)]
    return pl.pallas_call(
        _inproj0_kernel,
        grid=(T // tm,),
        in_specs=[row(D_MODEL), _resident((1, D_MODEL)), _resident(w.shape), _resident(wa2p.shape),
                  _resident((1, aw_k)), tab, tab, tab],
        out_specs=[row(n) for n, _ in outs],
        out_shape=[jax.ShapeDtypeStruct((T, n), dt) for n, dt in outs],
        compiler_params=_params("parallel"),
        name="inproj0",
    )(x2, g.reshape(1, -1), w, wa2p, ba.reshape(1, -1), cos, sin_lo, sin_hi)


def _gla_kernel(q_ref, k_ref, la_ref, v_ref, gate_ref, gn_ref, o_ref, st0_ref, st1_ref):
    C, G = GLA_CHUNK, GLA_GROUP
    seq = q_ref.shape[0]
    st_refs = (st0_ref, st1_ref)
    for st in st_refs:
        st[...] = jnp.zeros_like(st)
    head0 = _head0_lanes((C, LANES))
    r = lax.broadcasted_iota(jnp.int32, (C, C), 0)
    c = lax.broadcasted_iota(jnp.int32, (C, C), 1)
    tril = c <= r
    tri = jnp.where(tril, 1.0, 0.0).astype(BF16)
    tri3 = jnp.concatenate([tri, tri, tri], axis=1)
    gn = gn_ref[...]

    def body(it, _):
        rows = [pl.ds(pl.multiple_of((it * G + j) * C, C), C) for j in range(G)]
        la = jnp.concatenate([la_ref[rw, :] for rw in rows], axis=1)
        hi = la.astype(BF16)
        rem = la - hi.astype(F32)
        mid = rem.astype(BF16)
        lo = (rem - mid.astype(F32)).astype(BF16)
        bcum_all = _dot(tri3, jnp.concatenate([hi, mid, lo], axis=0))
        chains, decs = [], []
        for j in range(G):
            bcum = bcum_all[:, j * LANES:(j + 1) * LANES]
            btot = bcum[C - 1:C, :]
            q = q_ref[rows[j], :].astype(F32)
            k = k_ref[rows[j], :].astype(F32)
            q_dec = q * jnp.exp(bcum)
            k_inv = (k * jnp.exp(-bcum)).astype(BF16)
            k_tail = k * jnp.exp(btot - bcum)
            decs.append(jnp.exp(btot))
            for h in range(2):
                lanes_h = head0 if h == 0 else jnp.logical_not(head0)
                qh = jnp.where(lanes_h, q_dec, 0.0).astype(BF16)
                kth = jnp.where(lanes_h, k_tail, 0.0).astype(BF16)
                vh = v_ref[rows[j], h * GLA_DV:(h + 1) * GLA_DV]
                chains.append((j, h, qh, k_inv, kth, vh))
        scores = [jnp.where(tril, _dot_nt(qh, k_inv), 0.0).astype(BF16) for _, _, qh, k_inv, _, _ in chains]
        incs = [_dot_tn(vh, kth) for _, _, _, _, kth, vh in chains]
        states = {}
        for h in range(2):
            st = st_refs[h][...]
            for j in range(G):
                states[j, h] = st.astype(BF16)
                st = st * decs[j] + incs[2 * j + h]
            st_refs[h][...] = st
        outs = [_dot(sc, vh) + _dot_nt(qh, states[j, h])
                for sc, (j, h, qh, _, _, vh) in zip(scores, chains)]
        for o, (j, h, _, _, _, _) in zip(outs, chains):
            cols = slice(h * GLA_DV, (h + 1) * GLA_DV)
            y = _rmsnorm(o, gn) * _silu(gate_ref[rows[j], cols])
            o_ref[rows[j], cols] = y.astype(o_ref.dtype)
        return 0

    lax.fori_loop(0, seq // (C * G), body, 0)


def _gla(aq, ak, la, av, ag, gla_norm, batch, seq):
    npairs = GLA_HEADS // 2
    qk = pl.BlockSpec((seq, LANES), lambda b, p: (b, p))
    vv = pl.BlockSpec((seq, 2 * GLA_DV), lambda b, p: (b, p))
    return pl.pallas_call(
        _gla_kernel,
        grid=(batch, npairs),
        in_specs=[qk, qk, qk, vv, vv, _resident((1, GLA_DV))],
        out_specs=vv,
        out_shape=jax.ShapeDtypeStruct((batch * seq, GLA_HEADS * GLA_DV), BF16),
        scratch_shapes=[pltpu.VMEM((GLA_DV, LANES), F32), pltpu.VMEM((GLA_DV, LANES), F32)],
        compiler_params=_params("parallel", "parallel"),
        name="gla",
    )(aq, ak, la, av, ag, gla_norm.reshape(1, -1))


def _dsw_kernel(q_ref, k_ref, v_ref, o_ref, *stat_refs):
    C, G = DSW_BLOCK, DSW_GROUP
    seq = q_ref.shape[0]
    npat = len(DSW_PATTERNS)
    m_refs, l_refs, a_refs = stat_refs[0:npat], stat_refs[npat:2 * npat], stat_refs[2 * npat:3 * npat]
    head0 = _head0_lanes((C, LANES))
    qi = lax.broadcasted_iota(jnp.int32, (C, 2 * C), 0)
    ki = lax.broadcasted_iota(jnp.int32, (C, 2 * C), 1)
    valid_cur = jnp.logical_and(ki >= C, ki - C <= qi)
    valid_both = jnp.logical_or(jnp.logical_and(ki < C, ki >= qi), valid_cur)

    def rows(start, d):
        return pl.ds(start, C) if d == 1 else pl.ds(start, C, stride=d)

    def score_stage(pat, d, nb, it):
        base = it * G
        if nb > 1:
            assert nb % G == 0
            stream, blk0 = base // nb, base % nb
            starts = [stream + d * C * (blk0 + j) for j in range(G)]
            first = rows(max(starts[0] - d * C, 0), d)
            kb = [k_ref[first, :].astype(BF16)] + [k_ref[rows(s, d), :].astype(BF16) for s in starts]
            vb = [v_ref[first, :].astype(BF16)] + [v_ref[rows(s, d), :].astype(BF16) for s in starts]
            kcats = [jnp.concatenate([kb[j], kb[j + 1]], axis=0) for j in range(G)]
            vcats = [jnp.concatenate([vb[j], vb[j + 1]], axis=0) for j in range(G)]
            valids = [valid_both if blk0 + j > 0 else valid_cur for j in range(G)]
        else:
            starts = [base + j for j in range(G)]
            kcats = [k_ref[rows(s, d), :].astype(BF16) for s in starts]
            vcats = [v_ref[rows(s, d), :].astype(BF16) for s in starts]
            valids = [valid_cur[:, C:]] * G
        scores = []
        for j in range(G):
            q = q_ref[rows(starts[j], d), :]
            for h in range(2):
                lanes_h = head0 if h == 0 else jnp.logical_not(head0)
                scores.append(_dot_nt(jnp.where(lanes_h, q, 0.0).astype(BF16), kcats[j]))
        return pat, d, starts, scores, vcats, valids

    def value_stage(pat, d, starts, scores, vcats, valids):
        ones = jnp.ones((vcats[0].shape[0], LANES), BF16)
        vexts = [jnp.concatenate([v, ones], axis=1) for v in vcats]
        stats = []
        for i, s in enumerate(scores):
            s = jnp.where(valids[i // 2], s, MASK_NEG)
            m = jnp.max(s, axis=-1, keepdims=True)
            stats.append((m, jnp.exp((s - m).astype(BF16))))
        nums = [_dot(p, vexts[i // 2]) for i, (_, p) in enumerate(stats)]
        for j in range(G):
            cur = rows(starts[j], d)
            n0, n1 = nums[2 * j], nums[2 * j + 1]
            m_refs[pat][cur, :] = jnp.where(head0, stats[2 * j][0], stats[2 * j + 1][0])
            l_refs[pat][cur, :] = jnp.where(head0, n0[:, LANES:], n1[:, LANES:])
            a_refs[pat][cur, :] = jnp.where(head0, n0[:, :LANES], n1[:, :LANES])

    groups = []
    for pat, (window, d) in enumerate(DSW_PATTERNS):
        assert window // d == C
        groups += [(pat, d, seq // d // C, it) for it in range(seq // C // G)]
    pending = score_stage(*groups[0])
    for i in range(len(groups)):
        upcoming = score_stage(*groups[i + 1]) if i + 1 < len(groups) else None
        value_stage(*pending)
        pending = upcoming

    R = DSW_COMBINE_ROWS
    for r0 in range(0, seq, R):
        rows_r = slice(r0, r0 + R)
        ms = [m[rows_r, :] for m in m_refs]
        m_max = functools.reduce(jnp.maximum, ms)
        ws = [jnp.exp(m - m_max) for m in ms]
        num = sum(w * a[rows_r, :] for w, a in zip(ws, a_refs))
        den = sum(w * l[rows_r, :] for w, l in zip(ws, l_refs))
        o_ref[rows_r, :] = (num / den).astype(o_ref.dtype)


def _dsw(bq, bk, bv, batch, seq):
    npairs = DSW_HEADS // 2
    spec = pl.BlockSpec((seq, LANES), lambda b, p: (b, p))
    return pl.pallas_call(
        _dsw_kernel,
        grid=(batch, npairs),
        in_specs=[spec, spec, spec],
        out_specs=spec,
        out_shape=jax.ShapeDtypeStruct((batch * seq, DSW_HEADS * HEAD_DIM), BF16),
        scratch_shapes=[pltpu.VMEM((seq, LANES), F32) for _ in range(3 * len(DSW_PATTERNS))],
        compiler_params=_params("parallel", "parallel"),
        name="dsw",
    )(bq, bk, bv)


def _ffn_kernel(h_ref, ma_ref, mb_ref, wo_ref, gf_ref, wu_ref, wc_ref, wd_ref, gfin_ref,
                o_ref, carry_ref, ubuf_ref, xn_ref, act_ref, *, tiles_per_seq, final_norm):
    tm = h_ref.shape[0]
    half = ma_ref.shape[1]
    tf = FFN_TF
    nchunks = D_FF // tf
    H = SUBLANES

    @pl.when(pl.program_id(0) % tiles_per_seq == 0)
    def _():
        carry_ref[...] = jnp.zeros_like(carry_ref)

    h1 = h_ref[...] + _dot(ma_ref[...], wo_ref[0:half, :]) + _dot(mb_ref[...], wo_ref[half:, :])
    xn_ref[...] = _rmsnorm(h1, gf_ref[...]).astype(BF16)
    o_ref[...] = h1

    def up(c):
        for slot in range(2):
            buf = ubuf_ref.at[c % 2, slot]
            buf[0:H, :] = carry_ref[c, slot]
            buf[H:, :] = _dot(xn_ref[...], wu_ref[:, slot * D_FF + c * tf:slot * D_FF + (c + 1) * tf])
            carry_ref[c, slot] = buf[tm:tm + H, :]

    def conv(c, slot):
        buf = ubuf_ref.at[c % 2, slot]
        w = wc_ref[:, slot * D_FF + c * tf:slot * D_FF + (c + 1) * tf]
        out = w[FFN_CONV - 1:FFN_CONV, :] * buf[H:, :]
        for j in range(1, FFN_CONV):
            out = out + w[FFN_CONV - 1 - j:FFN_CONV - j, :] * buf[H - j:H - j + tm, :]
        return out

    up(0)
    for c in range(nchunks):
        if c + 1 < nchunks:
            up(c + 1)
        act_ref[:, c * tf:(c + 1) * tf] = (_silu(conv(c, 0)) * conv(c, 1)).astype(BF16)
    out = o_ref[...] + _dot(act_ref[...], wd_ref[...])
    if final_norm:
        out = _rmsnorm(out, gfin_ref[...])
    o_ref[...] = out


def _outproj_ffn(h2, mix_a, mix_b, w_out, g_ffn, w_up, w_conv, w_down, g_final, seq, final_norm):
    T = h2.shape[0]
    tm, tf = FFN_TM, FFN_TF
    nchunks = D_FF // tf
    half = mix_a.shape[1]
    wc = jnp.concatenate([w_conv, jnp.zeros((SUBLANES - FFN_CONV, 2 * D_FF), F32)], axis=0)
    wd = w_down.astype(BF16)
    row = lambda n: pl.BlockSpec((tm, n), lambda i: (i, 0))
    kern = functools.partial(_ffn_kernel, tiles_per_seq=seq // tm, final_norm=final_norm)
    return pl.pallas_call(
        kern,
        grid=(T // tm,),
        in_specs=[row(D_MODEL), row(half), row(half), _resident((2 * half, D_MODEL)), _resident((1, D_MODEL)),
                  _resident(w_up.shape), _resident(wc.shape), _resident(wd.shape), _resident((1, D_MODEL))],
        out_specs=row(D_MODEL),
        out_shape=jax.ShapeDtypeStruct((T, D_MODEL), F32),
        scratch_shapes=[pltpu.VMEM((nchunks, 2, SUBLANES, tf), F32),
                        pltpu.VMEM((2, 2, tm + SUBLANES, tf), F32),
                        pltpu.VMEM((tm, D_MODEL), BF16),
                        pltpu.VMEM((tm, D_FF), BF16)],
        compiler_params=_params("arbitrary"),
        name="outproj_ffn",
    )(h2, mix_a, mix_b, w_out.astype(BF16), g_ffn.reshape(1, -1), w_up.astype(BF16), wc, wd,
      g_final.reshape(1, -1))


def _inproj1_kernel(x_ref, g_ref, w_ref, cw_ref, b_ref, lg_ref, lb_ref, oc_ref, q_ref, k_ref, v_ref, buf_ref, sh_ref):
    ts = x_ref.shape[0]
    HALO = CONV_HALO
    xn = _rmsnorm(x_ref[...], g_ref[...]).astype(BF16)

    def proj(lo, hi):
        return _dot(xn, w_ref[:, lo:hi])

    @pl.when(pl.program_id(1) == 0)
    def _():
        buf_ref[0:HALO, :] = jnp.zeros((HALO, CONV_CH), F32)

    @pl.when(pl.program_id(1) > 0)
    def _():
        buf_ref[0:HALO, :] = buf_ref[ts:ts + HALO, :]

    buf_ref[HALO:, :] = proj(0, 512) * jax.nn.sigmoid(proj(512, 1024))
    n_sh = ts + HALO - SUBLANES
    for b in range(1, SUBLANES):
        sh_ref[b, 0:n_sh, :] = buf_ref[b:b + n_sh, :]
    bias, lg, lb = b_ref[...], lg_ref[...], lb_ref[...]
    R = CONV_RB

    def zero_after(ref):
        bits = lax.bitcast_convert_type(ref[0:SUBLANES, 0:LANES].astype(F32), jnp.uint32)
        return lax.bitcast_convert_type(lax.shift_right_logical(bits, jnp.uint32(32)), F32)

    def conv_block(rb, after=None):
        r0 = rb * R
        accs = [jnp.zeros((R // SUBLANES, SUBLANES, CONV_CH), F32) for _ in range(2)]
        for kk in range(CONV_WIDTH):
            off = HALO - (CONV_WIDTH - 1) + kk
            a, b = off // SUBLANES, off % SUBLANES
            rows = pl.ds(r0 + a * SUBLANES, R)
            x = buf_ref[rows, :] if b == 0 else sh_ref[b, rows, :]
            accs[kk % 2] = accs[kk % 2] + cw_ref[kk][None] * x.reshape(R // SUBLANES, SUBLANES, CONV_CH)
        acc = (accs[0] + accs[1]).reshape(R, CONV_CH)
        y = acc + bias
        if after is not None:
            y = y + jnp.tile(zero_after(after), (R // SUBLANES, CONV_CH // LANES))
        mu = jnp.mean(y, axis=-1, keepdims=True)
        yc = y - mu
        var = jnp.mean(yc * yc, axis=-1, keepdims=True)
        z = yc * lax.rsqrt(var + EPS) * lg + lb
        oc_ref[pl.ds(r0, R), :] = _silu(z).astype(oc_ref.dtype)

    nrb = ts // R
    quarter = nrb // 4
    outputs = (None, q_ref, k_ref, v_ref)
    for seg in range(4):
        if seg == 0:
            q_ref[...] = (proj(1024, 1536) * (HEAD_DIM ** -0.5)).astype(BF16)
        elif seg == 1:
            k_ref[...] = proj(1536, 2048).astype(BF16)
        elif seg == 2:
            v_ref[...] = proj(2048, 2560).astype(BF16)
        for rb in range(seg * quarter, nrb if seg == 3 else (seg + 1) * quarter):
            conv_block(rb, after=outputs[seg])


def _inproj1_convmod(x2, g, w_in, conv_w, conv_b, ln_g, ln_b, batch, seq):
    T = x2.shape[0]
    ts = PROJ_TM
    spb = seq // ts
    wrep = jnp.broadcast_to(conv_w[:, None, :], (CONV_WIDTH, SUBLANES, CONV_CH))
    row = lambda n: pl.BlockSpec((ts, n), lambda b, s: (b * spb + s, 0))
    vec = _resident((1, CONV_CH))
    outs = [(CONV_CH, BF16), (512, BF16), (512, BF16), (512, BF16)]
    return pl.pallas_call(
        _inproj1_kernel,
        grid=(batch, spb),
        in_specs=[row(D_MODEL), _resident((1, D_MODEL)), _resident(w_in.shape), _resident(wrep.shape), vec, vec, vec],
        out_specs=[row(n) for n, _ in outs],
        out_shape=[jax.ShapeDtypeStruct((T, n), dt) for n, dt in outs],
        scratch_shapes=[pltpu.VMEM((CONV_HALO + ts, CONV_CH), F32),
                        pltpu.VMEM((SUBLANES, CONV_HALO + ts, CONV_CH), F32)],
        compiler_params=_params("parallel", "arbitrary"),
        name="inproj1_convmod",
    )(x2, g.reshape(1, -1), w_in.astype(BF16), wrep, conv_b.reshape(1, -1), ln_g.reshape(1, -1), ln_b.reshape(1, -1))


def _sb_kernel(q_ref, k_ref, v_ref, u_ref, o_ref, carry_ref, acc_ref):
    T = SB_TILE
    nblk = q_ref.shape[0] // T
    head0 = _head0_lanes((T, LANES))
    qpos = lax.broadcasted_iota(jnp.int32, (T, T), 0)
    kpos = lax.broadcasted_iota(jnp.int32, (T, T), 1)
    below_diag = kpos < qpos
    carry_ref[...] = jnp.zeros_like(carry_ref)
    acc_ref[...] = jnp.zeros_like(acc_ref)

    def score_stage(kb):
        rows = slice(kb * T, (kb + 1) * T)
        k_blk = k_ref[rows, :]
        chains = [(qb, h, qb == kb) for qb in range(kb, nblk) for h in range(2)]
        zs = []
        for qb, h, _ in chains:
            q = q_ref[qb * T:(qb + 1) * T, :]
            lanes_h = head0 if h == 0 else jnp.logical_not(head0)
            zs.append(_dot_nt(jnp.where(lanes_h, q, jnp.zeros_like(q)), k_blk))
        return chains, zs, v_ref[rows, :]

    def value_stage(chains, zs, v_blk):
        nls = []
        for (qb, h, masked), z in zip(chains, zs):
            neg_abs = lax.bitcast_convert_type(lax.bitcast_convert_type(z, jnp.uint32) | jnp.uint32(SIGN_BIT), F32)
            nl = jnp.maximum(z, 0.0) + jnp.log(1.0 + jnp.exp(neg_abs))
            if masked:
                nl = jnp.where(below_diag, nl, 0.0)
            nls.append(nl.astype(BF16))
        incls = [_dot(nl, u_ref[...]) for nl in nls]
        probs = []
        for (qb, h, masked), z, incl in zip(chains, zs, incls):
            carry = carry_ref[qb, h]
            a = jnp.exp((z - (incl + jnp.concatenate([carry] * (T // LANES), axis=1))).astype(BF16))
            if masked:
                a = jnp.where(below_diag, a, jnp.zeros_like(a))
            carry_ref[qb, h] = carry + jnp.broadcast_to(incl[:, 0:1], (T, LANES))
            probs.append(a)
        for (qb, h, _), a in zip(chains, probs):
            acc_ref[qb, h] += _dot(a, v_blk)

    pending = score_stage(nblk - 1)
    for kb in reversed(range(nblk)):
        upcoming = score_stage(kb - 1) if kb > 0 else None
        value_stage(*pending)
        pending = upcoming
    for qb in range(nblk):
        out = jnp.where(head0, acc_ref[qb, 0], acc_ref[qb, 1])
        o_ref[qb * T:(qb + 1) * T, :] = out.astype(o_ref.dtype)


def _sb(dq, dk, dv, batch, seq):
    T = SB_TILE
    npairs = SB_HEADS // 2
    nblk = seq // T
    j = lax.broadcasted_iota(jnp.int32, (T, T), 0)
    s = lax.broadcasted_iota(jnp.int32, (T, T), 1)
    u = jnp.where(j >= s, 1.0, 0.0).astype(BF16)
    spec = pl.BlockSpec((seq, LANES), lambda b, p: (b, p))
    return pl.pallas_call(
        _sb_kernel,
        grid=(batch, npairs),
        in_specs=[spec, spec, spec, _resident((T, T))],
        out_specs=spec,
        out_shape=jax.ShapeDtypeStruct((batch * seq, SB_HEADS * HEAD_DIM), BF16),
        scratch_shapes=[pltpu.VMEM((nblk, 2, T, LANES), F32), pltpu.VMEM((nblk, 2, T, LANES), F32)],
        compiler_params=_params("parallel", "parallel"),
        name="stickbreak",
    )(dq, dk, dv, u)


def kernel(x, norm_mix0, w_in0, gla_wa2, gla_ba, gla_norm, w_out0, norm_ffn0, ffn_up0, ffn_conv0, ffn_down0,
           norm_mix1, w_in1, conv_w1, conv_b1, conv_ln_g1, conv_ln_b1, w_out1, norm_ffn1, ffn_up1, ffn_conv1,
           ffn_down1, final_norm):
    batch, seq, d = x.shape
    h = x.reshape(batch * seq, d)
    aq, ak, av, ag, la, bq, bk, bv = _inproj0(h, norm_mix0, w_in0, gla_wa2, gla_ba, seq)
    oa = _gla(aq, ak, la, av, ag, gla_norm, batch, seq)
    ob = _dsw(bq, bk, bv, batch, seq)
    h = _outproj_ffn(h, oa, ob, w_out0, norm_ffn0, ffn_up0, ffn_conv0, ffn_down0, final_norm, seq, False)
    oc, dq, dk, dv = _inproj1_convmod(h, norm_mix1, w_in1, conv_w1, conv_b1, conv_ln_g1, conv_ln_b1, batch, seq)
    od = _sb(dq, dk, dv, batch, seq)
    h = _outproj_ffn(h, oc, od, w_out1, norm_ffn1, ffn_up1, ffn_conv1, ffn_down1, final_norm, seq, True)
    return h.reshape(batch, seq, d)
```
